```python
import jax, jax.numpy as jnp
from jax import lax
import numpy as np

D_MODEL = 1024
BATCH = 16
SEQ = 2048
DEPTH = 1

RWKV_HEADS = 8
RWKV_HEAD_DIM = 64
D_RWKV = RWKV_HEADS * RWKV_HEAD_DIM
DECAY_LORA = 64
ICLR_LORA = 64
GATE_LORA = 128
GN_EPS = 64e-5
L2_EPS = 1e-12
MLA_HEADS = 8
MLA_NOPE_DIM = 64
MLA_ROPE_DIM = 32
MLA_V_DIM = 64
D_MLA = MLA_HEADS * MLA_V_DIM
Q_LORA_RANK = 768
KV_LORA_RANK = 256
ROPE_THETA = 10000.0
MLA_SCALE = (MLA_NOPE_DIM + MLA_ROPE_DIM) ** -0.5
Q_BLOCK = 128
D_MIX = D_RWKV + D_MLA
RWKV_SPLITS = (D_RWKV, D_RWKV, D_RWKV, DECAY_LORA, DECAY_LORA, ICLR_LORA, ICLR_LORA, GATE_LORA)
MLA_SPLITS = (Q_LORA_RANK, KV_LORA_RANK, MLA_ROPE_DIM)
RWKV_COLS = 3 * D_RWKV + 2 * DECAY_LORA + 2 * ICLR_LORA + GATE_LORA
MLA_COLS = Q_LORA_RANK + KV_LORA_RANK + MLA_ROPE_DIM
D_IN_PROJ = RWKV_COLS + MLA_COLS
D_FF = 2816
CONV_WIDTH = 3
NORM_EPS = 1e-6

kernel_name = 'hymba_rwkv7_mla_convffn_bidir'


def split_cols(z, sizes):
    bounds = np.cumsum(sizes)[:-1].tolist()
    return jnp.split(z, bounds, axis=-1)


def rms_norm(x, g):
    xf = x.astype(jnp.float32)
    y = xf * lax.rsqrt(jnp.mean(xf * xf, axis=-1, keepdims=True) + NORM_EPS)
    return (y * g.astype(jnp.float32)).astype(x.dtype)


def centred_token_shift(z, mu_prev, mu_next):
    z_prev = jnp.pad(z[:, :-1], ((0, 0), (1, 0), (0, 0)))
    z_next = jnp.pad(z[:, 1:], ((0, 0), (0, 1), (0, 0)))
    return z + mu_prev * (z_prev - z) + mu_next * (z_next - z)


def wkv7_scan(r, w, k, v, kk, kka, reverse):
    B, T, H, N = r.shape

    def step(S, inp):
        r_t, w_t, k_t, v_t, kk_t, kka_t = inp
        s_kk = jnp.einsum('bhij,bhj->bhi', S, kk_t)
        S = (S * w_t[:, :, None, :]
             - s_kk[..., None] * kka_t[:, :, None, :]
             + v_t[..., None] * k_t[:, :, None, :])
        return S, jnp.einsum('bhij,bhj->bhi', S, r_t)

    xs = tuple(jnp.swapaxes(t, 0, 1) for t in (r, w, k, v, kk, kka))
    S0 = jnp.zeros((B, H, N, N), jnp.float32)
    _, ys = lax.scan(step, S0, xs, reverse=reverse)
    return jnp.swapaxes(ys, 0, 1)


def rwkv7_decay(wd, w0, w2):
    logit = w0 + jnp.tanh(wd) @ w2
    return jnp.exp(-jnp.exp(-jax.nn.softplus(-logit) - 0.5))


def rwkv7_bidir_mixer(z, mu_prev, mu_next, w0f, w2f, w0b, w2b, a0f, a2f, a0b, a2b,
                      g2, k_k, k_a, r_k, lnx_g, lnx_b):
    B, T, _ = z.shape
    H, N = RWKV_HEADS, RWKV_HEAD_DIM
    f32 = lambda t: t.astype(jnp.float32)
    heads = lambda t: t.reshape(B, T, H, N)
    z = centred_token_shift(f32(z), f32(mu_prev), f32(mu_next))
    r, k, v, wdf, wdb, adf, adb, gd = split_cols(z, RWKV_SPLITS)
    w_f = rwkv7_decay(wdf, f32(w0f), f32(w2f))
    w_b = rwkv7_decay(wdb, f32(w0b), f32(w2b))
    a_f = jax.nn.sigmoid(f32(a0f) + adf @ f32(a2f))
    a_b = jax.nn.sigmoid(f32(a0b) + adb @ f32(a2b))
    g = jax.nn.sigmoid(gd) @ f32(g2)
    kk = heads(k * f32(k_k))
    kk = kk / jnp.maximum(jnp.linalg.norm(kk, axis=-1, keepdims=True), L2_EPS)
    k_f = k * (1.0 + (a_f - 1.0) * f32(k_a))
    k_b = k * (1.0 + (a_b - 1.0) * f32(k_a))
    r_h, v_h = heads(r), heads(v)
    y = (wkv7_scan(r_h, heads(w_f), heads(k_f), v_h, kk, kk * heads(a_f), False)
         + wkv7_scan(r_h, heads(w_b), heads(k_b), v_h, kk, kk * heads(a_b), True))
    mu = jnp.mean(y, axis=-1, keepdims=True)
    var = jnp.mean(jnp.square(y - mu), axis=-1, keepdims=True)
    y = ((y - mu) * lax.rsqrt(var + GN_EPS)).reshape(B, T, D_RWKV) * f32(lnx_g) + f32(lnx_b)
    bonus = jnp.sum(r_h * heads(k_f + k_b) * f32(r_k), axis=-1, keepdims=True) * v_h
    return (y + bonus.reshape(B, T, D_RWKV)) * g


def rope_tables(T):
    inv_freq = jnp.power(ROPE_THETA, -jnp.arange(0, MLA_ROPE_DIM, 2, dtype=jnp.float32) / MLA_ROPE_DIM)
    ang = jnp.arange(T, dtype=jnp.float32)[:, None] * inv_freq[None, :]
    ang = jnp.concatenate([ang, ang], axis=-1)
    return jnp.cos(ang), jnp.sin(ang)


def apply_rope(x, cos, sin):
    x1, x2 = jnp.split(x, 2, axis=-1)
    rot = jnp.concatenate([-x2, x1], axis=-1)
    return x * cos.astype(x.dtype) + rot * sin.astype(x.dtype)


def bidir_block_attention(q_nope, q_rope, k_nope, k_rope, v):
    B, T, H, _ = q_nope.shape
    nb = T // Q_BLOCK
    to_blocks = lambda t: jnp.moveaxis(t.reshape(B, nb, Q_BLOCK, H, t.shape[-1]), 1, 0)

    def one_block(qs):
        qn, qr = qs
        s = (jnp.einsum('bqhd,bkhd->bhqk', qn, k_nope)
             + jnp.einsum('bqhr,bkr->bhqk', qr, k_rope))
        p = jax.nn.softmax(s.astype(jnp.float32) * MLA_SCALE, axis=-1).astype(v.dtype)
        return jnp.einsum('bhqk,bkhd->bqhd', p, v)

    o = lax.map(one_block, (to_blocks(q_nope), to_blocks(q_rope)))
    return jnp.moveaxis(o, 0, 1).reshape(B, T, H, MLA_V_DIM)


def mla_mixer(z, q_norm_g, w_uq, kv_norm_g, w_ukv, out_g, cos, sin):
    B, T, _ = z.shape
    H = MLA_HEADS
    c_q, c_kv, k_rope = split_cols(z, MLA_SPLITS)
    q = (rms_norm(c_q, q_norm_g) @ w_uq).reshape(B, T, H, MLA_NOPE_DIM + MLA_ROPE_DIM)
    q_nope = q[..., :MLA_NOPE_DIM]
    q_rope = apply_rope(q[..., MLA_NOPE_DIM:], cos[:, None, :], sin[:, None, :])
    kv = (rms_norm(c_kv, kv_norm_g) @ w_ukv).reshape(B, T, H, MLA_NOPE_DIM + MLA_V_DIM)
    k_nope, v = kv[..., :MLA_NOPE_DIM], kv[..., MLA_NOPE_DIM:]
    k_rope = apply_rope(k_rope, cos, sin)
    o = bidir_block_attention(q_nope, q_rope, k_nope, k_rope, v)
    return rms_norm(o.reshape(B, T, D_MLA), out_g)


def conv_ffn(n, w_up, conv_w, conv_b, w_down):
    u = n @ w_up
    pad = CONV_WIDTH // 2
    u = lax.conv_general_dilated(u, conv_w[:, None, :].astype(u.dtype), window_strides=(1,),
                                 padding=((pad, pad),), dimension_numbers=('NWC', 'WIO', 'NWC'),
                                 feature_group_count=2 * D_FF) + conv_b
    gate, val = jnp.split(u, 2, axis=-1)
    return (jax.nn.silu(gate) * val) @ w_down


def setup_inputs(seed: int = 0) -> dict:
    key = jax.random.key(seed)
    k = jax.random.split(key, 31)
    L = DEPTH
    nrm = lambda kk, shape, scale: scale * jax.random.normal(kk, shape, jnp.float32)
    gain = lambda kk, shape: 1.0 + 0.02 * jax.random.normal(kk, shape, jnp.float32)
    unif = lambda kk, shape, lo, hi: jax.random.uniform(kk, shape, jnp.float32, lo, hi)
    return {
        'x': nrm(k[0], (BATCH, SEQ, D_MODEL), 1.0),
        'ln_mix_g': gain(k[1], (L, D_MODEL)),
        'w_in': nrm(k[2], (L, D_MODEL, D_IN_PROJ), D_MODEL ** -0.5),
        'shift_mu_prev': unif(k[3], (L, RWKV_COLS), 0.0, 0.5),
        'shift_mu_next': unif(k[4], (L, RWKV_COLS), 0.0, 0.5),
        'decay_w0_fwd': unif(k[5], (L, D_RWKV), -6.5, -1.5),
        'decay_w2_fwd': nrm(k[6], (L, DECAY_LORA, D_RWKV), 0.5 * DECAY_LORA ** -0.5),
        'decay_w0_bwd': unif(k[7], (L, D_RWKV), -6.5, -1.5),
        'decay_w2_bwd': nrm(k[8], (L, DECAY_LORA, D_RWKV), 0.5 * DECAY_LORA ** -0.5),
        'iclr_a0_fwd': nrm(k[9], (L, D_RWKV), 0.5),
        'iclr_a2_fwd': nrm(k[10], (L, ICLR_LORA, D_RWKV), 0.5 * ICLR_LORA ** -0.5),
        'iclr_a0_bwd': nrm(k[11], (L, D_RWKV), 0.5),
        'iclr_a2_bwd': nrm(k[12], (L, ICLR_LORA, D_RWKV), 0.5 * ICLR_LORA ** -0.5),
        'gate_g2': nrm(k[13], (L, GATE_LORA, D_RWKV), GATE_LORA ** -0.5),
        'k_k': 0.85 + nrm(k[14], (L, D_RWKV), 0.02),
        'k_a': gain(k[15], (L, D_RWKV)),
        'r_k': nrm(k[16], (L, RWKV_HEADS, RWKV_HEAD_DIM), 0.1),
        'ln_x_g': gain(k[17], (L, D_RWKV)),
        'ln_x_b': nrm(k[18], (L, D_RWKV), 0.02),
        'q_norm_g': gain(k[19], (L, Q_LORA_RANK)),
        'w_uq': nrm(k[20], (L, Q_LORA_RANK, MLA_HEADS * (MLA_NOPE_DIM + MLA_ROPE_DIM)), Q_LORA_RANK ** -0.5),
        'kv_norm_g': gain(k[21], (L, KV_LORA_RANK)),
        'w_ukv': nrm(k[22], (L, KV_LORA_RANK, MLA_HEADS * (MLA_NOPE_DIM + MLA_V_DIM)), KV_LORA_RANK ** -0.5),
        'mla_out_g': gain(k[23], (L, D_MLA)),
        'w_out': nrm(k[24], (L, D_MIX, D_MODEL), D_MIX ** -0.5),
        'ln_ffn_g': gain(k[25], (L, D_MODEL)),
        'w_ffn_up': nrm(k[26], (L, D_MODEL, 2 * D_FF), D_MODEL ** -0.5),
        'ffn_conv_w': nrm(k[27], (L, CONV_WIDTH, 2 * D_FF), CONV_WIDTH ** -0.5),
        'ffn_conv_b': nrm(k[28], (L, 2 * D_FF), 0.02),
        'w_ffn_down': nrm(k[29], (L, D_FF, D_MODEL), D_FF ** -0.5),
        'ln_final_g': gain(k[30], (D_MODEL,)),
    }


def reference(x, ln_mix_g, w_in, shift_mu_prev, shift_mu_next, decay_w0_fwd, decay_w2_fwd,
              decay_w0_bwd, decay_w2_bwd, iclr_a0_fwd, iclr_a2_fwd, iclr_a0_bwd, iclr_a2_bwd,
              gate_g2, k_k, k_a, r_k, ln_x_g, ln_x_b, q_norm_g, w_uq, kv_norm_g, w_ukv,
              mla_out_g, w_out, ln_ffn_g, w_ffn_up, ffn_conv_w, ffn_conv_b, w_ffn_down, ln_final_g):
    T = x.shape[1]
    cos, sin = rope_tables(T)
    h = x
    for l in range(DEPTH):
        n = rms_norm(h, ln_mix_g[l])
        z = n @ w_in[l]
        z_rwkv, z_mla = z[..., :RWKV_COLS], z[..., RWKV_COLS:]
        y_rwkv = rwkv7_bidir_mixer(z_rwkv, shift_mu_prev[l], shift_mu_next[l],
                                   decay_w0_fwd[l], decay_w2_fwd[l], decay_w0_bwd[l], decay_w2_bwd[l],
                                   iclr_a0_fwd[l], iclr_a2_fwd[l], iclr_a0_bwd[l], iclr_a2_bwd[l],
                                   gate_g2[l], k_k[l], k_a[l], r_k[l], ln_x_g[l], ln_x_b[l]).astype(h.dtype)
        y_mla = mla_mixer(z_mla, q_norm_g[l], w_uq[l], kv_norm_g[l], w_ukv[l], mla_out_g[l], cos, sin)
        h = h + jnp.concatenate([y_rwkv, y_mla], axis=-1) @ w_out[l]
        h = h + conv_ffn(rms_norm(h, ln_ffn_g[l]), w_ffn_up[l], ffn_conv_w[l], ffn_conv_b[l], w_ffn_down[l])
    return rms_norm(h, ln_final_g)
```

```python
import functools
import math

import numpy as np
import jax
import jax.numpy as jnp
from jax import lax
from jax.experimental import pallas as pl
from jax.experimental.pallas import tpu as pltpu

F32 = jnp.float32
BF16 = jnp.bfloat16

D_MODEL = 1024
RWKV_HEADS = 8
HEAD_DIM = 64
D_RWKV = RWKV_HEADS * HEAD_DIM
DECAY_LORA = 64
ICLR_LORA = 64
GATE_LORA = 128
GN_EPS = 64e-5
L2_EPS = 1e-12
MLA_HEADS = 8
MLA_NOPE_DIM = 64
MLA_ROPE_DIM = 32
MLA_V_DIM = 64
D_MLA = MLA_HEADS * MLA_V_DIM
Q_LORA_RANK = 768
KV_LORA_RANK = 256
ROPE_THETA = 10000.0
MLA_SCALE = (MLA_NOPE_DIM + MLA_ROPE_DIM) ** -0.5
RWKV_COLS = 3 * D_RWKV + 2 * DECAY_LORA + 2 * ICLR_LORA + GATE_LORA
MLA_COLS_PAD = 1152
D_FF = 2816
NORM_EPS = 1e-6
QK_PAD = 128

VMEM_LIMIT = 52 * 1024 * 1024

WKV_CHUNK = 64
DECAY_LOG_SCALE = -math.exp(-0.5)


def _dot(a, b):
    return jnp.dot(a.astype(BF16), b.astype(BF16), preferred_element_type=F32)


def _dot_nt(a, b):
    return lax.dot_general(a.astype(BF16), b.astype(BF16), (((1,), (1,)), ((), ())),
                           preferred_element_type=F32)


def _dot_tn(a, b):
    return lax.dot_general(a.astype(BF16), b.astype(BF16), (((0,), (0,)), ((), ())),
                           preferred_element_type=F32)


def _dot_hilo(a, b):
    hi = a.astype(BF16)
    lo = (a - hi.astype(F32)).astype(BF16)
    return (jnp.dot(hi, b, preferred_element_type=F32)
            + jnp.dot(lo, b, preferred_element_type=F32))


def _sigmoid(x):
    return 1.0 / (1.0 + jnp.exp(-x))


def _rms(x, g):
    return x * lax.rsqrt(jnp.mean(x * x, axis=-1, keepdims=True) + NORM_EPS) * g


def _inproj_kernel(x_ref, g_ref, wr_ref, wm_ref, zr_ref, zm_ref):
    n = _rms(x_ref[...], g_ref[...]).astype(BF16)
    zr_ref[...] = jnp.dot(n, wr_ref[...], preferred_element_type=F32)
    zm_ref[...] = jnp.dot(n, wm_ref[...], preferred_element_type=F32)


def _inproj(x2, g, w_r, w_m, tm):
    n_tok = x2.shape[0]
    full = lambda a: pl.BlockSpec(a.shape, lambda i: (0,) * a.ndim)
    return pl.pallas_call(
        _inproj_kernel,
        grid=(n_tok // tm,),
        in_specs=[pl.BlockSpec((tm, D_MODEL), lambda i: (i, 0)), full(g), full(w_r), full(w_m)],
        out_specs=[pl.BlockSpec((tm, RWKV_COLS), lambda i: (i, 0)),
                   pl.BlockSpec((tm, MLA_COLS_PAD), lambda i: (i, 0))],
        out_shape=[jax.ShapeDtypeStruct((n_tok, RWKV_COLS), F32),
                   jax.ShapeDtypeStruct((n_tok, MLA_COLS_PAD), F32)],
        compiler_params=pltpu.CompilerParams(dimension_semantics=("parallel",),
                                             vmem_limit_bytes=VMEM_LIMIT),
        name="inproj",
    )(x2, g, w_r, w_m)


def _rwkv_prep_kernel(z_ref, zp_ref, zn_ref, mup_ref, mun_ref, w0_ref, w2_ref, a0_ref, a2_ref,
                      g2_ref, kk_ref, ka_ref, rk_ref, ones_ref,
                      r_out, k_out, v_out, kkn_out, af_out, ab_out, lwf_out, lwb_out,
                      bonus_out, g_out):
    i = pl.program_id(1)
    nt = pl.num_programs(1)
    z = z_ref[0]
    tt = z.shape[0]
    row = lax.broadcasted_iota(jnp.int32, (tt, 1), 0)
    has_prev = (i > 0).astype(F32)
    has_next = (i < nt - 1).astype(F32)
    prev_row = zp_ref[0][7:8, :] * has_prev
    next_row = zn_ref[0][0:1, :] * has_next
    z_prev = jnp.where(row == 0, prev_row, pltpu.roll(z, 1, 0))
    z_next = jnp.where(row == tt - 1, next_row, pltpu.roll(z, tt - 1, 0))
    zs = z + mup_ref[...] * (z_prev - z) + mun_ref[...] * (z_next - z)

    d = D_RWKV
    r = zs[:, 0:d]
    k = zs[:, d:2 * d]
    v = zs[:, 2 * d:3 * d]
    wd = zs[:, 3 * d:3 * d + 128]
    ad = zs[:, 3 * d + 128:3 * d + 256]
    gd = zs[:, 3 * d + 256:3 * d + 384]

    logit = w0_ref[...] + _dot(jnp.tanh(wd), w2_ref[...])
    logw = DECAY_LOG_SCALE * _sigmoid(logit)
    a = _sigmoid(a0_ref[...] + _dot(ad, a2_ref[...]))
    g = _dot(_sigmoid(gd), g2_ref[...])
    a_f = a[:, :d]
    a_b = a[:, d:]

    ones_bd = ones_ref[...]
    kk = k * kk_ref[...]
    ss = _dot_hilo(kk * kk, ones_bd)
    kk = kk / jnp.maximum(jnp.sqrt(ss), L2_EPS)
    k_a = ka_ref[...]
    k_fb = k * (2.0 + (a_f + a_b - 2.0) * k_a)
    bonus = _dot_hilo(r * k_fb * rk_ref[...], ones_bd) * v

    r_out[0] = r.astype(r_out.dtype)
    k_out[0] = k.astype(k_out.dtype)
    v_out[0] = v.astype(v_out.dtype)
    kkn_out[0] = kk.astype(kkn_out.dtype)
    af_out[0] = a_f.astype(af_out.dtype)
    ab_out[0] = a_b.astype(ab_out.dtype)
    lwf_out[0] = logw[:, :d]
    lwb_out[0] = logw[:, d:]
    bonus_out[0] = bonus
    g_out[0] = g


def _rwkv_prep(z_r, params, tt):
    B, T, _ = z_r.shape
    nt = T // tt
    hb = tt // 8
    full = lambda a: pl.BlockSpec(a.shape, lambda b, i: (0,) * a.ndim)
    seq = lambda w: pl.BlockSpec((1, tt, w), lambda b, i: (b, i, 0))
    in_specs = [
        seq(RWKV_COLS),
        pl.BlockSpec((1, 8, RWKV_COLS), lambda b, i: (b, jnp.maximum(i * hb - 1, 0), 0)),
        pl.BlockSpec((1, 8, RWKV_COLS), lambda b, i: (b, jnp.minimum((i + 1) * hb, T // 8 - 1), 0)),
    ] + [full(p) for p in params]
    out_dtypes = [F32] * 10
    return pl.pallas_call(
        _rwkv_prep_kernel,
        grid=(B, nt),
        in_specs=in_specs,
        out_specs=[seq(D_RWKV) for _ in out_dtypes],
        out_shape=[jax.ShapeDtypeStruct((B, T, D_RWKV), dt) for dt in out_dtypes],
        compiler_params=pltpu.CompilerParams(dimension_semantics=("parallel", "parallel"),
                                             vmem_limit_bytes=VMEM_LIMIT),
        name="rwkv_prep",
    )(z_r, z_r, z_r, *params)


def _wkv_chunk(rows, reverse, r_ref, k_ref, v_ref, kk_ref, a_ref, lw_ref, ka_ref, y_ref, s_ref,
               n_heads):
    C = WKV_CHUNK
    N = HEAD_DIM
    lw = lw_ref[0, rows, :]
    ti = lax.broadcasted_iota(jnp.int32, (C, C), 0)
    si = lax.broadcasted_iota(jnp.int32, (C, C), 1)
    if reverse:
        incl = si >= ti
        strict = si > ti
    else:
        incl = si <= ti
        strict = si < ti
    tri = incl.astype(BF16)
    lw_hi = lw.astype(BF16)
    lw_lo = (lw - lw_hi.astype(F32)).astype(BF16)
    cum = (jnp.dot(tri, lw_hi, preferred_element_type=F32)
           + jnp.dot(tri, lw_lo, preferred_element_type=F32))
    total = jnp.sum(lw, axis=0, keepdims=True)
    p_end = jnp.exp(total)
    e_inc = jnp.exp(cum)
    e_dec = jnp.exp(-cum)
    e_exc = jnp.exp(cum - lw)

    r = r_ref[0, rows, :].astype(F32)
    k = k_ref[0, rows, :].astype(F32)
    v = v_ref[0, rows, :].astype(F32)
    kk = kk_ref[0, rows, :].astype(F32)
    a = a_ref[0, rows, :].astype(F32)
    k_dir = k * (1.0 + (a - 1.0) * ka_ref[...])
    alpha = e_exc * kk
    beta = e_dec * kk * a
    kappa = e_dec * k_dir
    rho = e_inc * r
    eye = (ti == si).astype(F32)

    ys = []
    for h in range(n_heads):
        sl = slice(h * N, (h + 1) * N)
        al, be, ka, rh, vh = alpha[:, sl], beta[:, sl], kappa[:, sl], rho[:, sl], v[:, sl]
        gram = _dot_nt(jnp.concatenate([al, rh], axis=0), jnp.concatenate([be, ka], axis=0))
        L = jnp.where(strict, gram[:C, :C], 0.0)
        AK = jnp.where(strict, gram[:C, C:], 0.0)
        RB = jnp.where(incl, gram[C:, :C], 0.0)
        RK = jnp.where(incl, gram[C:, C:], 0.0)
        tinv = eye - L
        lp = L
        for _ in range(int(math.log2(C)) - 1):
            lp = _dot(lp, lp)
            tinv = tinv + _dot(tinv, lp)
        akv = _dot(AK, vh)
        wu = _dot(tinv, jnp.concatenate([al, akv], axis=1))
        S = s_ref[h]
        x = _dot_nt(jnp.concatenate([wu[:, :N], rh], axis=0), S)
        U = x[:C] + wu[:, N:]
        y = x[C:] + _dot(jnp.concatenate([RK, -RB], axis=1), jnp.concatenate([vh, U], axis=0))
        s_new = S + _dot_tn(jnp.concatenate([vh, -U], axis=0), jnp.concatenate([ka, be], axis=0))
        s_ref[h] = s_new * p_end[:, sl]
        ys.append(y)
    y_ref[0, rows, :] = jnp.concatenate(ys, axis=1)


def _wkv_kernel(rf_ref, kf_ref, vf_ref, kkf_ref, af_ref, lwf_ref,
                rb_ref, kb_ref, vb_ref, kkb_ref, ab_ref, lwb_ref, ka_ref,
                yf_ref, yb_ref, sf_ref, sb_ref, *, n_heads, n_chunks):
    @pl.when(pl.program_id(2) == 0)
    def _():
        sf_ref[...] = jnp.zeros_like(sf_ref)
        sb_ref[...] = jnp.zeros_like(sb_ref)

    C = WKV_CHUNK

    def body(j, carry):
        rows_f = pl.ds(pl.multiple_of(j * C, C), C)
        rows_b = pl.ds(pl.multiple_of((n_chunks - 1 - j) * C, C), C)
        _wkv_chunk(rows_f, False, rf_ref, kf_ref, vf_ref, kkf_ref, af_ref, lwf_ref, ka_ref,
                   yf_ref, sf_ref, n_heads)
        _wkv_chunk(rows_b, True, rb_ref, kb_ref, vb_ref, kkb_ref, ab_ref, lwb_ref, ka_ref,
                   yb_ref, sb_ref, n_heads)
        return carry

    lax.fori_loop(0, n_chunks, body, 0)


def _wkv(r, k, v, kk, a_f, a_b, lw_f, lw_b, k_a, tt, hg):
    B, T, _ = r.shape
    nt = T // tt
    ng = RWKV_HEADS // hg
    w = hg * HEAD_DIM
    fwd = pl.BlockSpec((1, tt, w), lambda b, g, i: (b, i, g))
    bwd = pl.BlockSpec((1, tt, w), lambda b, g, i: (b, nt - 1 - i, g))
    ka_spec = pl.BlockSpec((1, w), lambda b, g, i: (0, g))
    kern = functools.partial(_wkv_kernel, n_heads=hg, n_chunks=tt // WKV_CHUNK)
    return pl.pallas_call(
        kern,
        grid=(B, ng, nt),
        in_specs=[fwd] * 6 + [bwd] * 6 + [ka_spec],
        out_specs=[fwd, bwd],
        out_shape=[jax.ShapeDtypeStruct((B, T, D_RWKV), F32)] * 2,
        scratch_shapes=[pltpu.VMEM((hg, HEAD_DIM, HEAD_DIM), F32)] * 2,
        compiler_params=pltpu.CompilerParams(
            dimension_semantics=("parallel", "parallel", "arbitrary"),
            vmem_limit_bytes=VMEM_LIMIT),
        name="wkv",
    )(r, k, v, kk, a_f, lw_f, r, k, v, kk, a_b, lw_b, k_a)


def _mla_prep_kernel(z_ref, qg_ref, kvg_ref, wq_ref, wqr_ref, wk_ref, wvt_ref, e2_ref,
                     cosq_ref, sinq_ref, cs_ref, q_out, k_out, vt_out):
    z = z_ref[0]
    c_q = z[:, :Q_LORA_RANK]
    c_kv = z[:, Q_LORA_RANK:Q_LORA_RANK + KV_LORA_RANK]
    kr = z[:, Q_LORA_RANK + KV_LORA_RANK:]
    nq = _rms(c_q, qg_ref[...]).astype(BF16)
    nkv = _rms(c_kv, kvg_ref[...]).astype(BF16)
    q = (jnp.dot(nq, wq_ref[...], preferred_element_type=F32) * cosq_ref[...]
         + jnp.dot(nq, wqr_ref[...], preferred_element_type=F32) * sinq_ref[...])
    kf = (jnp.dot(nkv, wk_ref[...], preferred_element_type=F32)
          + _dot(kr * cs_ref[...], e2_ref[...]))
    vt = lax.dot_general(wvt_ref[...], nkv, (((1,), (1,)), ((), ())), preferred_element_type=F32)
    for h in range(MLA_HEADS):
        q_out[0, h] = q[:, h * QK_PAD:(h + 1) * QK_PAD].astype(q_out.dtype)
        k_out[0, h] = kf[:, h * QK_PAD:(h + 1) * QK_PAD].astype(k_out.dtype)
    vt_out[0] = vt.astype(vt_out.dtype)


def _mla_prep(z_m, params, tabs, tt):
    B, T, _ = z_m.shape
    full = lambda a: pl.BlockSpec(a.shape, lambda b, i: (0,) * a.ndim)
    tab = lambda a: pl.BlockSpec((tt, a.shape[1]), lambda b, i: (i, 0))
    hspec = pl.BlockSpec((1, MLA_HEADS, tt, QK_PAD), lambda b, i: (b, 0, i, 0))
    return pl.pallas_call(
        _mla_prep_kernel,
        grid=(B, T // tt),
        in_specs=[pl.BlockSpec((1, tt, MLA_COLS_PAD), lambda b, i: (b, i, 0))]
        + [full(p) for p in params] + [tab(t) for t in tabs],
        out_specs=[hspec, hspec, pl.BlockSpec((1, D_MLA, tt), lambda b, i: (b, 0, i))],
        out_shape=[jax.ShapeDtypeStruct((B, MLA_HEADS, T, QK_PAD), BF16),
                   jax.ShapeDtypeStruct((B, MLA_HEADS, T, QK_PAD), BF16),
                   jax.ShapeDtypeStruct((B, D_MLA, T), BF16)],
        compiler_params=pltpu.CompilerParams(dimension_semantics=("parallel", "parallel"),
                                             vmem_limit_bytes=VMEM_LIMIT),
        name="mla_prep",
    )(z_m, *params, *tabs)


def _attn_kernel(q_ref, k_ref, vt_ref, og_ref, o_ref):
    outs = []
    for h in range(MLA_HEADS):
        s = _dot_nt(k_ref[0, h], q_ref[0, h]) * MLA_SCALE
        m = jnp.max(s, axis=0, keepdims=True)
        p = jnp.exp(s - m)
        l = jnp.sum(p, axis=0, keepdims=True)
        o = jnp.dot(vt_ref[0, h * MLA_V_DIM:(h + 1) * MLA_V_DIM, :], p.astype(BF16),
                    preferred_element_type=F32)
        outs.append(o / l)
    o = jnp.concatenate(outs, axis=0)
    y = o * lax.rsqrt(jnp.mean(o * o, axis=0, keepdims=True) + NORM_EPS) * og_ref[...]
    o_ref[0] = y.T


def _attn(q, k, vt, out_g_col, tq):
    B, H, T, _ = q.shape
    return pl.pallas_call(
        _attn_kernel,
        grid=(B, T // tq),
        in_specs=[pl.BlockSpec((1, H, tq, QK_PAD), lambda b, i: (b, 0, i, 0)),
                  pl.BlockSpec((1, H, T, QK_PAD), lambda b, i: (b, 0, 0, 0)),
                  pl.BlockSpec((1, D_MLA, T), lambda b, i: (b, 0, 0)),
                  pl.BlockSpec(out_g_col.shape, lambda b, i: (0, 0))],
        out_specs=pl.BlockSpec((1, tq, D_MLA), lambda b, i: (b, i, 0)),
        out_shape=jax.ShapeDtypeStruct((B, T, D_MLA), F32),
        compiler_params=pltpu.CompilerParams(dimension_semantics=("parallel", "arbitrary"),
                                             vmem_limit_bytes=VMEM_LIMIT),
        name="attn",
    )(q, k, vt, out_g_col)


def _mix_out_kernel(yf_ref, yb_ref, bonus_ref, g_ref, ym_ref, x_ref, lng_ref, lnb_ref, ones_ref,
                    wo_r_ref, wo_m_ref, fg_ref, h_out, n_out):
    y = yf_ref[...] + yb_ref[...]
    ones_bd = ones_ref[...]
    inv_n = 1.0 / HEAD_DIM
    mu = _dot_hilo(y, ones_bd) * inv_n
    yc = y - mu
    var = _dot_hilo(yc * yc, ones_bd) * inv_n
    gn = yc * lax.rsqrt(var + GN_EPS) * lng_ref[...] + lnb_ref[...]
    y_rwkv = (gn + bonus_ref[...]) * g_ref[...]
    h = (x_ref[...] + _dot(y_rwkv, wo_r_ref[...]) + _dot(ym_ref[...], wo_m_ref[...]))
    h_out[...] = h
    n_out[...] = _rms(h, fg_ref[...]).astype(n_out.dtype)


def _mix_out(y_f, y_b, bonus, g, y_mla, x2, params, tm):
    n_tok = x2.shape[0]
    full = lambda a: pl.BlockSpec(a.shape, lambda i: (0,) * a.ndim)
    tok = lambda w: pl.BlockSpec((tm, w), lambda i: (i, 0))
    return pl.pallas_call(
        _mix_out_kernel,
        grid=(n_tok // tm,),
        in_specs=[tok(D_RWKV)] * 4 + [tok(D_MLA), tok(D_MODEL)] + [full(p) for p in params],
        out_specs=[tok(D_MODEL), tok(D_MODEL)],
        out_shape=[jax.ShapeDtypeStruct((n_tok, D_MODEL), F32),
                   jax.ShapeDtypeStruct((n_tok, D_MODEL), BF16)],
        compiler_params=pltpu.CompilerParams(dimension_semantics=("parallel",),
                                             vmem_limit_bytes=VMEM_LIMIT),
        name="mix_out",
    )(y_f, y_b, bonus, g, y_mla, x2, *params)


def _ffn_kernel(n_ref, wg_ref, wv_ref, cg_ref, cv_ref, bg_ref, bv_ref, wd_ref, o_ref):
    j = pl.program_id(1)
    n = n_ref[0]
    T = n.shape[0]
    row = lax.broadcasted_iota(jnp.int32, (T, 1), 0)
    not_first = (row > 0).astype(F32)
    not_last = (row < T - 1).astype(F32)

    def conv(u, c_ref, b_ref):
        c = c_ref[...]
        u_prev = pltpu.roll(u, 1, 0) * not_first
        u_next = pltpu.roll(u, T - 1, 0) * not_last
        return c[0:1] * u_prev + c[1:2] * u + c[2:3] * u_next + b_ref[...]

    gate = conv(jnp.dot(n, wg_ref[...], preferred_element_type=F32), cg_ref, bg_ref)
    val = conv(jnp.dot(n, wv_ref[...], preferred_element_type=F32), cv_ref, bv_ref)
    act = (gate * _sigmoid(gate) * val).astype(BF16)
    contrib = jnp.dot(act, wd_ref[...], preferred_element_type=F32)

    @pl.when(j == 0)
    def _():
        o_ref[0] = contrib

    @pl.when(j > 0)
    def _():
        o_ref[0] += contrib


def _ffn(n2, w_gate, w_val, c_gate, c_val, b_gate, b_val, w_down, fc):
    B, T, _ = n2.shape
    nj = D_FF // fc
    return pl.pallas_call(
        _ffn_kernel,
        grid=(B, nj),
        in_specs=[pl.BlockSpec((1, T, D_MODEL), lambda b, j: (b, 0, 0)),
                  pl.BlockSpec((D_MODEL, fc), lambda b, j: (0, j)),
                  pl.BlockSpec((D_MODEL, fc), lambda b, j: (0, j)),
                  pl.BlockSpec((3, fc), lambda b, j: (0, j)),
                  pl.BlockSpec((3, fc), lambda b, j: (0, j)),
                  pl.BlockSpec((1, fc), lambda b, j: (0, j)),
                  pl.BlockSpec((1, fc), lambda b, j: (0, j)),
                  pl.BlockSpec((fc, D_MODEL), lambda b, j: (j, 0))],
        out_specs=pl.BlockSpec((1, T, D_MODEL), lambda b, j: (b, 0, 0)),
        out_shape=jax.ShapeDtypeStruct((B, T, D_MODEL), F32),
        compiler_params=pltpu.CompilerParams(dimension_semantics=("parallel", "arbitrary"),
                                             vmem_limit_bytes=VMEM_LIMIT),
        name="ffn",
    )(n2, w_gate, w_val, c_gate, c_val, b_gate, b_val, w_down)


def _final_kernel(h_ref, f_ref, g_ref, o_ref):
    o_ref[...] = _rms(h_ref[...] + f_ref[...], g_ref[...])


def _final(h1, f, g, tm):
    n_tok = h1.shape[0]
    tok = pl.BlockSpec((tm, D_MODEL), lambda i: (i, 0))
    return pl.pallas_call(
        _final_kernel,
        grid=(n_tok // tm,),
        in_specs=[tok, tok, pl.BlockSpec(g.shape, lambda i: (0, 0))],
        out_specs=tok,
        out_shape=jax.ShapeDtypeStruct((n_tok, D_MODEL), F32),
        compiler_params=pltpu.CompilerParams(dimension_semantics=("parallel",),
                                             vmem_limit_bytes=VMEM_LIMIT),
        name="final",
    )(h1, f, g)


def _rot_cols(w):
    half = w.shape[-1] // 2
    return jnp.concatenate([-w[..., half:], w[..., :half]], axis=-1)


def _block_diag2(a, b):
    za = jnp.zeros((a.shape[0], b.shape[1]), a.dtype)
    zb = jnp.zeros((b.shape[0], a.shape[1]), b.dtype)
    return jnp.concatenate([jnp.concatenate([a, za], axis=1), jnp.concatenate([zb, b], axis=1)], axis=0)


def _head_ones():
    idx = np.arange(D_RWKV) // HEAD_DIM
    return jnp.asarray((idx[:, None] == idx[None, :]).astype(np.float32), dtype=BF16)


def _rope_tables(T):
    inv_freq = jnp.power(ROPE_THETA, -jnp.arange(0, MLA_ROPE_DIM, 2, dtype=F32) / MLA_ROPE_DIM)
    ang = jnp.arange(T, dtype=F32)[:, None] * inv_freq[None, :]
    ang = jnp.concatenate([ang, ang], axis=-1)
    cos, sin = jnp.cos(ang), jnp.sin(ang)
    one = jnp.ones((T, MLA_NOPE_DIM), F32)
    zn = jnp.zeros((T, MLA_NOPE_DIM), F32)
    zp = jnp.zeros((T, QK_PAD - MLA_NOPE_DIM - MLA_ROPE_DIM), F32)
    cosq = jnp.tile(jnp.concatenate([one, cos, zp], axis=1), (1, MLA_HEADS))
    sinq = jnp.tile(jnp.concatenate([zn, sin, zp], axis=1), (1, MLA_HEADS))
    cs = jnp.concatenate([cos, sin, jnp.zeros((T, 128 - 2 * MLA_ROPE_DIM), F32)], axis=1)
    return cosq, sinq, cs


def _pad_heads(w, src_width, src_off, n, dst_off):
    K = w.shape[0]
    w = w.reshape(K, MLA_HEADS, src_width)[:, :, src_off:src_off + n]
    out = jnp.zeros((K, MLA_HEADS, QK_PAD), w.dtype)
    out = out.at[:, :, dst_off:dst_off + n].set(w)
    return out.reshape(K, MLA_HEADS * QK_PAD)


def kernel(x, ln_mix_g, w_in, shift_mu_prev, shift_mu_next, decay_w0_fwd, decay_w2_fwd, decay_w0_bwd, decay_w2_bwd, iclr_a0_fwd, iclr_a2_fwd, iclr_a0_bwd, iclr_a2_bwd, gate_g2, k_k, k_a, r_k, ln_x_g, ln_x_b, q_norm_g, w_uq, kv_norm_g, w_ukv, mla_out_g, w_out, ln_ffn_g, w_ffn_up, ffn_conv_w, ffn_conv_b, w_ffn_down, ln_final_g):
    B, T, D = x.shape
    n_tok = B * T
    row = lambda a: a.reshape(1, -1).astype(F32)
    l = 0

    w_in_l = w_in[l]
    w_r = w_in_l[:, :RWKV_COLS].astype(BF16)
    w_kr = w_in_l[:, RWKV_COLS + Q_LORA_RANK + KV_LORA_RANK:]
    w_m = jnp.concatenate([w_in_l[:, RWKV_COLS:], _rot_cols(w_kr),
                           jnp.zeros((D, MLA_COLS_PAD - (Q_LORA_RANK + KV_LORA_RANK + 2 * MLA_ROPE_DIM)), F32)],
                          axis=1).astype(BF16)
    ones_bd = _head_ones()
    prep_params = [
        row(shift_mu_prev[l]), row(shift_mu_next[l]),
        row(jnp.concatenate([decay_w0_fwd[l], decay_w0_bwd[l]])),
        _block_diag2(decay_w2_fwd[l], decay_w2_bwd[l]).astype(BF16),
        row(jnp.concatenate([iclr_a0_fwd[l], iclr_a0_bwd[l]])),
        _block_diag2(iclr_a2_fwd[l], iclr_a2_bwd[l]).astype(BF16),
        gate_g2[l].astype(BF16),
        row(k_k[l]), row(k_a[l]), row(r_k[l]), ones_bd,
    ]
    qw = MLA_NOPE_DIM + MLA_ROPE_DIM
    w_uq_l = w_uq[l]
    wq_pad = (_pad_heads(w_uq_l, qw, 0, MLA_NOPE_DIM, 0)
              + _pad_heads(w_uq_l, qw, MLA_NOPE_DIM, MLA_ROPE_DIM, MLA_NOPE_DIM)).astype(BF16)
    rope_cols = w_uq_l.reshape(Q_LORA_RANK, MLA_HEADS, qw)[:, :, MLA_NOPE_DIM:]
    rot_full = jnp.concatenate([jnp.zeros((Q_LORA_RANK, MLA_HEADS, MLA_NOPE_DIM), F32), _rot_cols(rope_cols)],
                               axis=-1).reshape(Q_LORA_RANK, MLA_HEADS * qw)
    wq_rot_pad = _pad_heads(rot_full, qw, MLA_NOPE_DIM, MLA_ROPE_DIM, MLA_NOPE_DIM).astype(BF16)
    kvw = MLA_NOPE_DIM + MLA_V_DIM
    w_ukv_l = w_ukv[l]
    wk_pad = _pad_heads(w_ukv_l, kvw, 0, MLA_NOPE_DIM, 0).astype(BF16)
    wv_t = w_ukv_l.reshape(KV_LORA_RANK, MLA_HEADS, kvw)[:, :, MLA_NOPE_DIM:].reshape(KV_LORA_RANK, D_MLA).T.astype(BF16)
    e2 = np.zeros((128, MLA_HEADS * QK_PAD), np.float32)
    for h in range(MLA_HEADS):
        for j in range(MLA_ROPE_DIM):
            e2[j, h * QK_PAD + MLA_NOPE_DIM + j] = 1.0
            e2[MLA_ROPE_DIM + j, h * QK_PAD + MLA_NOPE_DIM + j] = 1.0
    e2 = jnp.asarray(e2, dtype=BF16)
    cosq, sinq, cs = _rope_tables(T)
    mla_params = [row(q_norm_g[l]), row(kv_norm_g[l]), wq_pad, wq_rot_pad, wk_pad, wv_t, e2]
    w_out_l = w_out[l]
    mix_params = [row(ln_x_g[l]), row(ln_x_b[l]), ones_bd,
                  w_out_l[:D_RWKV].astype(BF16), w_out_l[D_RWKV:].astype(BF16), row(ln_ffn_g[l])]
    w_up = w_ffn_up[l]
    conv_w = ffn_conv_w[l]
    conv_b = ffn_conv_b[l]

    x2 = x.reshape(n_tok, D)
    z_r, z_m = _inproj(x2, row(ln_mix_g[l]), w_r, w_m, tm=512)
    z_r = z_r.reshape(B, T, RWKV_COLS)
    z_m = z_m.reshape(B, T, MLA_COLS_PAD)

    r, k, v, kk, a_f, a_b, lw_f, lw_b, bonus, g = _rwkv_prep(z_r, prep_params, tt=min(256, T))
    y_f, y_b = _wkv(r, k, v, kk, a_f, a_b, lw_f, lw_b, row(k_a[l]), tt=min(256, T), hg=4)

    q, kf, vt = _mla_prep(z_m, mla_params, [cosq, sinq, cs], tt=min(512, T))
    y_mla = _attn(q, kf, vt, mla_out_g[l].reshape(D_MLA, 1).astype(F32), tq=min(256, T))

    flat = lambda a: a.reshape(n_tok, a.shape[-1])
    h1, n2 = _mix_out(flat(y_f), flat(y_b), flat(bonus), flat(g), flat(y_mla), x2, mix_params, tm=512)

    f = _ffn(n2.reshape(B, T, D), w_up[:, :D_FF].astype(BF16), w_up[:, D_FF:].astype(BF16),
             conv_w[:, :D_FF], conv_w[:, D_FF:], row(conv_b[:D_FF]), row(conv_b[D_FF:]),
             w_ffn_down[l].astype(BF16), fc=256)
    out = _final(h1, f.reshape(n_tok, D), row(ln_final_g), tm=1024)
    return out.reshape(B, T, D)
```

```python
import functools
import math

import numpy as np
import jax
import jax.numpy as jnp
from jax import lax
from jax.experimental import pallas as pl
from jax.experimental.pallas import tpu as pltpu

F32 = jnp.float32
BF16 = jnp.bfloat16

D_MODEL = 1024
RWKV_HEADS = 8
HEAD_DIM = 64
D_RWKV = RWKV_HEADS * HEAD_DIM
DECAY_LORA = 64
ICLR_LORA = 64
GATE_LORA = 128
GN_EPS = 64e-5
L2_EPS = 1e-12
MLA_HEADS = 8
MLA_NOPE_DIM = 64
MLA_ROPE_DIM = 32
MLA_V_DIM = 64
D_MLA = MLA_HEADS * MLA_V_DIM
Q_LORA_RANK = 768
KV_LORA_RANK = 256
ROPE_THETA = 10000.0
MLA_SCALE = (MLA_NOPE_DIM + MLA_ROPE_DIM) ** -0.5
RWKV_COLS = 3 * D_RWKV + 2 * DECAY_LORA + 2 * ICLR_LORA + GATE_LORA
MLA_COLS_PAD = 1152
D_FF = 2816
NORM_EPS = 1e-6
QK_PAD = 128

VMEM_LIMIT = 52 * 1024 * 1024

WKV_CHUNK = 64
DECAY_LOG_SCALE = -math.exp(-0.5)


def _dot(a, b):
    return jnp.dot(a.astype(BF16), b.astype(BF16), preferred_element_type=F32)


def _dot_nt(a, b):
    return lax.dot_general(a.astype(BF16), b.astype(BF16), (((1,), (1,)), ((), ())),
                           preferred_element_type=F32)


def _dot_tn(a, b):
    return lax.dot_general(a.astype(BF16), b.astype(BF16), (((0,), (0,)), ((), ())),
                           preferred_element_type=F32)


def _dot_hilo(a, b):
    hi = a.astype(BF16)
    lo = (a - hi.astype(F32)).astype(BF16)
    return (jnp.dot(hi, b, preferred_element_type=F32)
            + jnp.dot(lo, b, preferred_element_type=F32))


def _sigmoid(x):
    return 1.0 / (1.0 + jnp.exp(-x))


def _rms(x, g):
    return x * lax.rsqrt(jnp.mean(x * x, axis=-1, keepdims=True) + NORM_EPS) * g


def _inproj_kernel(x_ref, g_ref, wr_ref, wm_ref, zr_ref, zm_ref):
    n = _rms(x_ref[...], g_ref[...]).astype(BF16)
    zr_ref[...] = jnp.dot(n, wr_ref[...], preferred_element_type=F32)
    zm_ref[...] = jnp.dot(n, wm_ref[...], preferred_element_type=F32)


def _inproj(x2, g, w_r, w_m, tm):
    n_tok = x2.shape[0]
    full = lambda a: pl.BlockSpec(a.shape, lambda i: (0,) * a.ndim)
    return pl.pallas_call(
        _inproj_kernel,
        grid=(n_tok // tm,),
        in_specs=[pl.BlockSpec((tm, D_MODEL), lambda i: (i, 0)), full(g), full(w_r), full(w_m)],
        out_specs=[pl.BlockSpec((tm, RWKV_COLS), lambda i: (i, 0)),
                   pl.BlockSpec((tm, MLA_COLS_PAD), lambda i: (i, 0))],
        out_shape=[jax.ShapeDtypeStruct((n_tok, RWKV_COLS), F32),
                   jax.ShapeDtypeStruct((n_tok, MLA_COLS_PAD), F32)],
        compiler_params=pltpu.CompilerParams(dimension_semantics=("parallel",),
                                             vmem_limit_bytes=VMEM_LIMIT),
        name="inproj",
    )(x2, g, w_r, w_m)


def _rwkv_prep_kernel(z_ref, zp_ref, zn_ref, mup_ref, mun_ref, w0_ref, w2_ref, a0_ref, a2_ref,
                      g2_ref, kk_ref, ka_ref, rk_ref, ones_ref,
                      r_out, k_out, v_out, kkn_out, af_out, ab_out, lwf_out, lwb_out,
                      bonus_out, g_out):
    i = pl.program_id(1)
    nt = pl.num_programs(1)
    z = z_ref[0]
    tt = z.shape[0]
    row = lax.broadcasted_iota(jnp.int32, (tt, 1), 0)
    has_prev = (i > 0).astype(F32)
    has_next = (i < nt - 1).astype(F32)
    prev_row = zp_ref[0][7:8, :] * has_prev
    next_row = zn_ref[0][0:1, :] * has_next
    z_prev = jnp.where(row == 0, prev_row, pltpu.roll(z, 1, 0))
    z_next = jnp.where(row == tt - 1, next_row, pltpu.roll(z, tt - 1, 0))
    zs = z + mup_ref[...] * (z_prev - z) + mun_ref[...] * (z_next - z)

    d = D_RWKV
    r = zs[:, 0:d]
    k = zs[:, d:2 * d]
    v = zs[:, 2 * d:3 * d]
    wd = zs[:, 3 * d:3 * d + 128]
    ad = zs[:, 3 * d + 128:3 * d + 256]
    gd = zs[:, 3 * d + 256:3 * d + 384]

    logit = w0_ref[...] + _dot(jnp.tanh(wd), w2_ref[...])
    logw = DECAY_LOG_SCALE * _sigmoid(logit)
    a = _sigmoid(a0_ref[...] + _dot(ad, a2_ref[...]))
    g = _dot(_sigmoid(gd), g2_ref[...])
    a_f = a[:, :d]
    a_b = a[:, d:]

    ones_bd = ones_ref[...]
    kk = k * kk_ref[...]
    ss = _dot_hilo(kk * kk, ones_bd)
    kk = kk / jnp.maximum(jnp.sqrt(ss), L2_EPS)
    k_a = ka_ref[...]
    k_fb = k * (2.0 + (a_f + a_b - 2.0) * k_a)
    bonus = _dot_hilo(r * k_fb * rk_ref[...], ones_bd) * v

    r_out[0] = r.astype(r_out.dtype)
    k_out[0] = k.astype(k_out.dtype)
    v_out[0] = v.astype(v_out.dtype)
    kkn_out[0] = kk.astype(kkn_out.dtype)
    af_out[0] = a_f.astype(af_out.dtype)
    ab_out[0] = a_b.astype(ab_out.dtype)
    lwf_out[0] = logw[:, :d]
    lwb_out[0] = logw[:, d:]
    bonus_out[0] = bonus
    g_out[0] = g


def _rwkv_prep(z_r, params, tt):
    B, T, _ = z_r.shape
    nt = T // tt
    hb = tt // 8
    full = lambda a: pl.BlockSpec(a.shape, lambda b, i: (0,) * a.ndim)
    seq = lambda w: pl.BlockSpec((1, tt, w), lambda b, i: (b, i, 0))
    in_specs = [
        seq(RWKV_COLS),
        pl.BlockSpec((1, 8, RWKV_COLS), lambda b, i: (b, jnp.maximum(i * hb - 1, 0), 0)),
        pl.BlockSpec((1, 8, RWKV_COLS), lambda b, i: (b, jnp.minimum((i + 1) * hb, T // 8 - 1), 0)),
    ] + [full(p) for p in params]
    out_dtypes = [F32] * 10
    return pl.pallas_call(
        _rwkv_prep_kernel,
        grid=(B, nt),
        in_specs=in_specs,
        out_specs=[seq(D_RWKV) for _ in out_dtypes],
        out_shape=[jax.ShapeDtypeStruct((B, T, D_RWKV), dt) for dt in out_dtypes],
        compiler_params=pltpu.CompilerParams(dimension_semantics=("parallel", "parallel"),
                                             vmem_limit_bytes=VMEM_LIMIT),
        name="rwkv_prep",
    )(z_r, z_r, z_r, *params)


def _wkv_operands(rows, reverse, r_ref, k_ref, v_ref, kk_ref, a_ref, lw_ref, ka_ref):
    C = WKV_CHUNK
    lw = lw_ref[0, rows, :]
    ti = lax.broadcasted_iota(jnp.int32, (C, C), 0)
    si = lax.broadcasted_iota(jnp.int32, (C, C), 1)
    if reverse:
        incl = si >= ti
        strict = si > ti
    else:
        incl = si <= ti
        strict = si < ti
    tri = incl.astype(BF16)
    lw_hi = lw.astype(BF16)
    lw_lo = (lw - lw_hi.astype(F32)).astype(BF16)
    cum = (jnp.dot(tri, lw_hi, preferred_element_type=F32)
           + jnp.dot(tri, lw_lo, preferred_element_type=F32))
    e_inc = jnp.exp(cum)
    e_dec = jnp.exp(-cum)
    e_exc = jnp.exp(cum - lw)
    r = r_ref[0, rows, :].astype(F32)
    k = k_ref[0, rows, :].astype(F32)
    kk = kk_ref[0, rows, :].astype(F32)
    a = a_ref[0, rows, :].astype(F32)
    k_dir = k * (1.0 + (a - 1.0) * ka_ref[...])
    return dict(
        incl=incl, strict=strict, eye=(ti == si).astype(F32),
        p_end=jnp.exp(jnp.sum(lw, axis=0, keepdims=True)),
        alpha=e_exc * kk, beta=e_dec * kk * a, kappa=e_dec * k_dir, rho=e_inc * r,
        v=v_ref[0, rows, :].astype(F32))


def _wkv_step(dirs, n_heads):
    C = WKV_CHUNK
    N = HEAD_DIM
    chains = []
    for d in dirs:
        ops = d["ops"]
        for h in range(n_heads):
            sl = slice(h * N, (h + 1) * N)
            chains.append(dict(d=d, h=h, sl=sl, al=ops["alpha"][:, sl], be=ops["beta"][:, sl],
                               ka=ops["kappa"][:, sl], rh=ops["rho"][:, sl], vh=ops["v"][:, sl]))
    for c in chains:
        c["gram"] = _dot_nt(jnp.concatenate([c["al"], c["rh"]], axis=0),
                            jnp.concatenate([c["be"], c["ka"]], axis=0))
    for c in chains:
        ops, g = c["d"]["ops"], c["gram"]
        c["L"] = jnp.where(ops["strict"], g[:C, :C], 0.0)
        c["AK"] = jnp.where(ops["strict"], g[:C, C:], 0.0)
        c["RKB"] = jnp.concatenate([jnp.where(ops["incl"], g[C:, C:], 0.0),
                                    jnp.where(ops["incl"], -g[C:, :C], 0.0)], axis=1)
        c["tinv"] = ops["eye"] - c["L"]
        c["lp"] = c["L"]
    for c in chains:
        c["akv"] = _dot(c["AK"], c["vh"])
    for _ in range(int(math.log2(C)) - 1):
        for c in chains:
            c["lp"] = _dot(c["lp"], c["lp"])
        for c in chains:
            c["tinv"] = c["tinv"] + _dot(c["tinv"], c["lp"])
    for c in chains:
        c["wu"] = _dot(c["tinv"], jnp.concatenate([c["al"], c["akv"]], axis=1))
    for c in chains:
        c["S"] = c["d"]["s_ref"][c["h"]]
        c["x"] = _dot_nt(jnp.concatenate([c["wu"][:, :N], c["rh"]], axis=0), c["S"])
    for c in chains:
        c["U"] = c["x"][:C] + c["wu"][:, N:]
        c["y"] = c["x"][C:] + _dot(c["RKB"], jnp.concatenate([c["vh"], c["U"]], axis=0))
        c["ds"] = _dot_tn(jnp.concatenate([c["vh"], -c["U"]], axis=0),
                          jnp.concatenate([c["ka"], c["be"]], axis=0))
    for c in chains:
        c["d"]["s_ref"][c["h"]] = (c["S"] + c["ds"]) * c["d"]["ops"]["p_end"][:, c["sl"]]
    for d in dirs:
        d["y_ref"][0, d["rows"], :] = jnp.concatenate(
            [c["y"] for c in chains if c["d"] is d], axis=1)


def _wkv_kernel(rf_ref, kf_ref, vf_ref, kkf_ref, af_ref, lwf_ref,
                rb_ref, kb_ref, vb_ref, kkb_ref, ab_ref, lwb_ref, ka_ref,
                yf_ref, yb_ref, sf_ref, sb_ref, *, n_heads, n_chunks):
    @pl.when(pl.program_id(2) == 0)
    def _():
        sf_ref[...] = jnp.zeros_like(sf_ref)
        sb_ref[...] = jnp.zeros_like(sb_ref)

    C = WKV_CHUNK

    def body(j, carry):
        rows_f = pl.ds(pl.multiple_of(j * C, C), C)
        rows_b = pl.ds(pl.multiple_of((n_chunks - 1 - j) * C, C), C)
        fwd = dict(rows=rows_f, y_ref=yf_ref, s_ref=sf_ref,
                   ops=_wkv_operands(rows_f, False, rf_ref, kf_ref, vf_ref, kkf_ref, af_ref, lwf_ref, ka_ref))
        bwd = dict(rows=rows_b, y_ref=yb_ref, s_ref=sb_ref,
                   ops=_wkv_operands(rows_b, True, rb_ref, kb_ref, vb_ref, kkb_ref, ab_ref, lwb_ref, ka_ref))
        _wkv_step([fwd, bwd], n_heads)
        return carry

    lax.fori_loop(0, n_chunks, body, 0)


def _wkv(r, k, v, kk, a_f, a_b, lw_f, lw_b, k_a, tt, hg):
    B, T, _ = r.shape
    nt = T // tt
    ng = RWKV_HEADS // hg
    w = hg * HEAD_DIM
    fwd = pl.BlockSpec((1, tt, w), lambda b, g, i: (b, i, g))
    bwd = pl.BlockSpec((1, tt, w), lambda b, g, i: (b, nt - 1 - i, g))
    ka_spec = pl.BlockSpec((1, w), lambda b, g, i: (0, g))
    kern = functools.partial(_wkv_kernel, n_heads=hg, n_chunks=tt // WKV_CHUNK)
    return pl.pallas_call(
        kern,
        grid=(B, ng, nt),
        in_specs=[fwd] * 6 + [bwd] * 6 + [ka_spec],
        out_specs=[fwd, bwd],
        out_shape=[jax.ShapeDtypeStruct((B, T, D_RWKV), F32)] * 2,
        scratch_shapes=[pltpu.VMEM((hg, HEAD_DIM, HEAD_DIM), F32)] * 2,
        compiler_params=pltpu.CompilerParams(
            dimension_semantics=("parallel", "parallel", "arbitrary"),
            vmem_limit_bytes=VMEM_LIMIT),
        name="wkv",
    )(r, k, v, kk, a_f, lw_f, r, k, v, kk, a_b, lw_b, k_a)


def _mla_prep_kernel(z_ref, qg_ref, kvg_ref, wq_ref, wqr_ref, wk_ref, wvt_ref, e2_ref,
                     cosq_ref, sinq_ref, cs_ref, q_out, k_out, vt_out):
    z = z_ref[0]
    c_q = z[:, :Q_LORA_RANK]
    c_kv = z[:, Q_LORA_RANK:Q_LORA_RANK + KV_LORA_RANK]
    kr = z[:, Q_LORA_RANK + KV_LORA_RANK:]
    nq = _rms(c_q, qg_ref[...]).astype(BF16)
    nkv = _rms(c_kv, kvg_ref[...]).astype(BF16)
    q = (jnp.dot(nq, wq_ref[...], preferred_element_type=F32) * cosq_ref[...]
         + jnp.dot(nq, wqr_ref[...], preferred_element_type=F32) * sinq_ref[...])
    kf = (jnp.dot(nkv, wk_ref[...], preferred_element_type=F32)
          + _dot(kr * cs_ref[...], e2_ref[...]))
    vt = lax.dot_general(wvt_ref[...], nkv, (((1,), (1,)), ((), ())), preferred_element_type=F32)
    for h in range(MLA_HEADS):
        q_out[0, h] = q[:, h * QK_PAD:(h + 1) * QK_PAD].astype(q_out.dtype)
        k_out[0, h] = kf[:, h * QK_PAD:(h + 1) * QK_PAD].astype(k_out.dtype)
    vt_out[0] = vt.astype(vt_out.dtype)


def _mla_prep(z_m, params, tabs, tt):
    B, T, _ = z_m.shape
    full = lambda a: pl.BlockSpec(a.shape, lambda b, i: (0,) * a.ndim)
    tab = lambda a: pl.BlockSpec((tt, a.shape[1]), lambda b, i: (i, 0))
    hspec = pl.BlockSpec((1, MLA_HEADS, tt, QK_PAD), lambda b, i: (b, 0, i, 0))
    return pl.pallas_call(
        _mla_prep_kernel,
        grid=(B, T // tt),
        in_specs=[pl.BlockSpec((1, tt, MLA_COLS_PAD), lambda b, i: (b, i, 0))]
        + [full(p) for p in params] + [tab(t) for t in tabs],
        out_specs=[hspec, hspec, pl.BlockSpec((1, D_MLA, tt), lambda b, i: (b, 0, i))],
        out_shape=[jax.ShapeDtypeStruct((B, MLA_HEADS, T, QK_PAD), BF16),
                   jax.ShapeDtypeStruct((B, MLA_HEADS, T, QK_PAD), BF16),
                   jax.ShapeDtypeStruct((B, D_MLA, T), BF16)],
        compiler_params=pltpu.CompilerParams(dimension_semantics=("parallel", "parallel"),
                                             vmem_limit_bytes=VMEM_LIMIT),
        name="mla_prep",
    )(z_m, *params, *tabs)


def _attn_kernel(q_ref, k_ref, vt_ref, og_ref, o_ref):
    outs = []
    for h in range(MLA_HEADS):
        s = _dot_nt(k_ref[0, h], q_ref[0, h]) * MLA_SCALE
        m = jnp.max(s, axis=0, keepdims=True)
        p = jnp.exp(s - m)
        l = jnp.sum(p, axis=0, keepdims=True)
        o = jnp.dot(vt_ref[0, h * MLA_V_DIM:(h + 1) * MLA_V_DIM, :], p.astype(BF16),
                    preferred_element_type=F32)
        outs.append(o / l)
    o = jnp.concatenate(outs, axis=0)
    y = o * lax.rsqrt(jnp.mean(o * o, axis=0, keepdims=True) + NORM_EPS) * og_ref[...]
    o_ref[0] = y.T


def _attn(q, k, vt, out_g_col, tq):
    B, H, T, _ = q.shape
    return pl.pallas_call(
        _attn_kernel,
        grid=(B, T // tq),
        in_specs=[pl.BlockSpec((1, H, tq, QK_PAD), lambda b, i: (b, 0, i, 0)),
                  pl.BlockSpec((1, H, T, QK_PAD), lambda b, i: (b, 0, 0, 0)),
                  pl.BlockSpec((1, D_MLA, T), lambda b, i: (b, 0, 0)),
                  pl.BlockSpec(out_g_col.shape, lambda b, i: (0, 0))],
        out_specs=pl.BlockSpec((1, tq, D_MLA), lambda b, i: (b, i, 0)),
        out_shape=jax.ShapeDtypeStruct((B, T, D_MLA), F32),
        compiler_params=pltpu.CompilerParams(dimension_semantics=("parallel", "arbitrary"),
                                             vmem_limit_bytes=VMEM_LIMIT),
        name="attn",
    )(q, k, vt, out_g_col)


def _mix_out_kernel(yf_ref, yb_ref, bonus_ref, g_ref, ym_ref, x_ref, lng_ref, lnb_ref, ones_ref,
                    wo_r_ref, wo_m_ref, fg_ref, h_out, n_out):
    y = yf_ref[...] + yb_ref[...]
    ones_bd = ones_ref[...]
    inv_n = 1.0 / HEAD_DIM
    mu = _dot_hilo(y, ones_bd) * inv_n
    yc = y - mu
    var = _dot_hilo(yc * yc, ones_bd) * inv_n
    gn = yc * lax.rsqrt(var + GN_EPS) * lng_ref[...] + lnb_ref[...]
    y_rwkv = (gn + bonus_ref[...]) * g_ref[...]
    h = (x_ref[...] + _dot(y_rwkv, wo_r_ref[...]) + _dot(ym_ref[...], wo_m_ref[...]))
    h_out[...] = h
    n_out[...] = _rms(h, fg_ref[...]).astype(n_out.dtype)


def _mix_out(y_f, y_b, bonus, g, y_mla, x2, params, tm):
    n_tok = x2.shape[0]
    full = lambda a: pl.BlockSpec(a.shape, lambda i: (0,) * a.ndim)
    tok = lambda w: pl.BlockSpec((tm, w), lambda i: (i, 0))
    return pl.pallas_call(
        _mix_out_kernel,
        grid=(n_tok // tm,),
        in_specs=[tok(D_RWKV)] * 4 + [tok(D_MLA), tok(D_MODEL)] + [full(p) for p in params],
        out_specs=[tok(D_MODEL), tok(D_MODEL)],
        out_shape=[jax.ShapeDtypeStruct((n_tok, D_MODEL), F32),
                   jax.ShapeDtypeStruct((n_tok, D_MODEL), BF16)],
        compiler_params=pltpu.CompilerParams(dimension_semantics=("parallel",),
                                             vmem_limit_bytes=VMEM_LIMIT),
        name="mix_out",
    )(y_f, y_b, bonus, g, y_mla, x2, *params)


def _ffn_kernel(n_ref, wg_ref, wv_ref, cg_ref, cv_ref, bg_ref, bv_ref, wd_ref, o_ref):
    j = pl.program_id(1)
    n = n_ref[0]
    T = n.shape[0]
    row = lax.broadcasted_iota(jnp.int32, (T, 1), 0)
    not_first = (row > 0).astype(F32)
    not_last = (row < T - 1).astype(F32)

    def conv(u, c_ref, b_ref):
        c = c_ref[...]
        u_prev = pltpu.roll(u, 1, 0) * not_first
        u_next = pltpu.roll(u, T - 1, 0) * not_last
        return c[0:1] * u_prev + c[1:2] * u + c[2:3] * u_next + b_ref[...]

    gate = conv(jnp.dot(n, wg_ref[...], preferred_element_type=F32), cg_ref, bg_ref)
    val = conv(jnp.dot(n, wv_ref[...], preferred_element_type=F32), cv_ref, bv_ref)
    act = (gate * _sigmoid(gate) * val).astype(BF16)
    contrib = jnp.dot(act, wd_ref[...], preferred_element_type=F32)

    @pl.when(j == 0)
    def _():
        o_ref[0] = contrib

    @pl.when(j > 0)
    def _():
        o_ref[0] += contrib


def _ffn(n2, w_gate, w_val, c_gate, c_val, b_gate, b_val, w_down, fc):
    B, T, _ = n2.shape
    nj = D_FF // fc
    return pl.pallas_call(
        _ffn_kernel,
        grid=(B, nj),
        in_specs=[pl.BlockSpec((1, T, D_MODEL), lambda b, j: (b, 0, 0)),
                  pl.BlockSpec((D_MODEL, fc), lambda b, j: (0, j)),
                  pl.BlockSpec((D_MODEL, fc), lambda b, j: (0, j)),
                  pl.BlockSpec((3, fc), lambda b, j: (0, j)),
                  pl.BlockSpec((3, fc), lambda b, j: (0, j)),
                  pl.BlockSpec((1, fc), lambda b, j: (0, j)),
                  pl.BlockSpec((1, fc), lambda b, j: (0, j)),
                  pl.BlockSpec((fc, D_MODEL), lambda b, j: (j, 0))],
        out_specs=pl.BlockSpec((1, T, D_MODEL), lambda b, j: (b, 0, 0)),
        out_shape=jax.ShapeDtypeStruct((B, T, D_MODEL), F32),
        compiler_params=pltpu.CompilerParams(dimension_semantics=("parallel", "arbitrary"),
                                             vmem_limit_bytes=VMEM_LIMIT),
        name="ffn",
    )(n2, w_gate, w_val, c_gate, c_val, b_gate, b_val, w_down)


def _final_kernel(h_ref, f_ref, g_ref, o_ref):
    o_ref[...] = _rms(h_ref[...] + f_ref[...], g_ref[...])


def _final(h1, f, g, tm):
    n_tok = h1.shape[0]
    tok = pl.BlockSpec((tm, D_MODEL), lambda i: (i, 0))
    return pl.pallas_call(
        _final_kernel,
        grid=(n_tok // tm,),
        in_specs=[tok, tok, pl.BlockSpec(g.shape, lambda i: (0, 0))],
        out_specs=tok,
        out_shape=jax.ShapeDtypeStruct((n_tok, D_MODEL), F32),
        compiler_params=pltpu.CompilerParams(dimension_semantics=("parallel",),
                                             vmem_limit_bytes=VMEM_LIMIT),
        name="final",
    )(h1, f, g)


def _rot_cols(w):
    half = w.shape[-1] // 2
    return jnp.concatenate([-w[..., half:], w[..., :half]], axis=-1)


def _block_diag2(a, b):
    za = jnp.zeros((a.shape[0], b.shape[1]), a.dtype)
    zb = jnp.zeros((b.shape[0], a.shape[1]), b.dtype)
    return jnp.concatenate([jnp.concatenate([a, za], axis=1), jnp.concatenate([zb, b], axis=1)], axis=0)


def _head_ones():
    idx = np.arange(D_RWKV) // HEAD_DIM
    return jnp.asarray((idx[:, None] == idx[None, :]).astype(np.float32), dtype=BF16)


def _rope_tables(T):
    inv_freq = jnp.power(ROPE_THETA, -jnp.arange(0, MLA_ROPE_DIM, 2, dtype=F32) / MLA_ROPE_DIM)
    ang = jnp.arange(T, dtype=F32)[:, None] * inv_freq[None, :]
    ang = jnp.concatenate([ang, ang], axis=-1)
    cos, sin = jnp.cos(ang), jnp.sin(ang)
    one = jnp.ones((T, MLA_NOPE_DIM), F32)
    zn = jnp.zeros((T, MLA_NOPE_DIM), F32)
    zp = jnp.zeros((T, QK_PAD - MLA_NOPE_DIM - MLA_ROPE_DIM), F32)
    cosq = jnp.tile(jnp.concatenate([one, cos, zp], axis=1), (1, MLA_HEADS))
    sinq = jnp.tile(jnp.concatenate([zn, sin, zp], axis=1), (1, MLA_HEADS))
    cs = jnp.concatenate([cos, sin, jnp.zeros((T, 128 - 2 * MLA_ROPE_DIM), F32)], axis=1)
    return cosq, sinq, cs


def _pad_heads(w, src_width, src_off, n, dst_off):
    K = w.shape[0]
    w = w.reshape(K, MLA_HEADS, src_width)[:, :, src_off:src_off + n]
    out = jnp.zeros((K, MLA_HEADS, QK_PAD), w.dtype)
    out = out.at[:, :, dst_off:dst_off + n].set(w)
    return out.reshape(K, MLA_HEADS * QK_PAD)


def kernel(x, ln_mix_g, w_in, shift_mu_prev, shift_mu_next, decay_w0_fwd, decay_w2_fwd, decay_w0_bwd, decay_w2_bwd, iclr_a0_fwd, iclr_a2_fwd, iclr_a0_bwd, iclr_a2_bwd, gate_g2, k_k, k_a, r_k, ln_x_g, ln_x_b, q_norm_g, w_uq, kv_norm_g, w_ukv, mla_out_g, w_out, ln_ffn_g, w_ffn_up, ffn_conv_w, ffn_conv_b, w_ffn_down, ln_final_g):
    B, T, D = x.shape
    n_tok = B * T
    row = lambda a: a.reshape(1, -1).astype(F32)
    l = 0

    w_in_l = w_in[l]
    w_r = w_in_l[:, :RWKV_COLS].astype(BF16)
    w_kr = w_in_l[:, RWKV_COLS + Q_LORA_RANK + KV_LORA_RANK:]
    w_m = jnp.concatenate([w_in_l[:, RWKV_COLS:], _rot_cols(w_kr),
                           jnp.zeros((D, MLA_COLS_PAD - (Q_LORA_RANK + KV_LORA_RANK + 2 * MLA_ROPE_DIM)), F32)],
                          axis=1).astype(BF16)
    ones_bd = _head_ones()
    prep_params = [
        row(shift_mu_prev[l]), row(shift_mu_next[l]),
        row(jnp.concatenate([decay_w0_fwd[l], decay_w0_bwd[l]])),
        _block_diag2(decay_w2_fwd[l], decay_w2_bwd[l]).astype(BF16),
        row(jnp.concatenate([iclr_a0_fwd[l], iclr_a0_bwd[l]])),
        _block_diag2(iclr_a2_fwd[l], iclr_a2_bwd[l]).astype(BF16),
        gate_g2[l].astype(BF16),
        row(k_k[l]), row(k_a[l]), row(r_k[l]), ones_bd,
    ]
    qw = MLA_NOPE_DIM + MLA_ROPE_DIM
    w_uq_l = w_uq[l]
    wq_pad = (_pad_heads(w_uq_l, qw, 0, MLA_NOPE_DIM, 0)
              + _pad_heads(w_uq_l, qw, MLA_NOPE_DIM, MLA_ROPE_DIM, MLA_NOPE_DIM)).astype(BF16)
    rope_cols = w_uq_l.reshape(Q_LORA_RANK, MLA_HEADS, qw)[:, :, MLA_NOPE_DIM:]
    rot_full = jnp.concatenate([jnp.zeros((Q_LORA_RANK, MLA_HEADS, MLA_NOPE_DIM), F32), _rot_cols(rope_cols)],
                               axis=-1).reshape(Q_LORA_RANK, MLA_HEADS * qw)
    wq_rot_pad = _pad_heads(rot_full, qw, MLA_NOPE_DIM, MLA_ROPE_DIM, MLA_NOPE_DIM).astype(BF16)
    kvw = MLA_NOPE_DIM + MLA_V_DIM
    w_ukv_l = w_ukv[l]
    wk_pad = _pad_heads(w_ukv_l, kvw, 0, MLA_NOPE_DIM, 0).astype(BF16)
    wv_t = w_ukv_l.reshape(KV_LORA_RANK, MLA_HEADS, kvw)[:, :, MLA_NOPE_DIM:].reshape(KV_LORA_RANK, D_MLA).T.astype(BF16)
    e2 = np.zeros((128, MLA_HEADS * QK_PAD), np.float32)
    for h in range(MLA_HEADS):
        for j in range(MLA_ROPE_DIM):
            e2[j, h * QK_PAD + MLA_NOPE_DIM + j] = 1.0
            e2[MLA_ROPE_DIM + j, h * QK_PAD + MLA_NOPE_DIM + j] = 1.0
    e2 = jnp.asarray(e2, dtype=BF16)
    cosq, sinq, cs = _rope_tables(T)
    mla_params = [row(q_norm_g[l]), row(kv_norm_g[l]), wq_pad, wq_rot_pad, wk_pad, wv_t, e2]
    w_out_l = w_out[l]
    mix_params = [row(ln_x_g[l]), row(ln_x_b[l]), ones_bd,
                  w_out_l[:D_RWKV].astype(BF16), w_out_l[D_RWKV:].astype(BF16), row(ln_ffn_g[l])]
    w_up = w_ffn_up[l]
    conv_w = ffn_conv_w[l]
    conv_b = ffn_conv_b[l]

    x2 = x.reshape(n_tok, D)
    z_r, z_m = _inproj(x2, row(ln_mix_g[l]), w_r, w_m, tm=512)
    z_r = z_r.reshape(B, T, RWKV_COLS)
    z_m = z_m.reshape(B, T, MLA_COLS_PAD)

    r, k, v, kk, a_f, a_b, lw_f, lw_b, bonus, g = _rwkv_prep(z_r, prep_params, tt=min(256, T))
    y_f, y_b = _wkv(r, k, v, kk, a_f, a_b, lw_f, lw_b, row(k_a[l]), tt=min(256, T), hg=8)

    q, kf, vt = _mla_prep(z_m, mla_params, [cosq, sinq, cs], tt=min(512, T))
    y_mla = _attn(q, kf, vt, mla_out_g[l].reshape(D_MLA, 1).astype(F32), tq=min(256, T))

    flat = lambda a: a.reshape(n_tok, a.shape[-1])
    h1, n2 = _mix_out(flat(y_f), flat(y_b), flat(bonus), flat(g), flat(y_mla), x2, mix_params, tm=512)

    f = _ffn(n2.reshape(B, T, D), w_up[:, :D_FF].astype(BF16), w_up[:, D_FF:].astype(BF16),
             conv_w[:, :D_FF], conv_w[:, D_FF:], row(conv_b[:D_FF]), row(conv_b[D_FF:]),
             w_ffn_down[l].astype(BF16), fc=256)
    out = _final(h1, f.reshape(n_tok, D), row(ln_final_g), tm=1024)
    return out.reshape(B, T, D)
```

```python
import functools
import math

import numpy as np
import jax
import jax.numpy as jnp
from jax import lax
from jax.experimental import pallas as pl
from jax.experimental.pallas import tpu as pltpu

F32 = jnp.float32
BF16 = jnp.bfloat16

D_MODEL = 1024
RWKV_HEADS = 8
HEAD_DIM = 64
D_RWKV = RWKV_HEADS * HEAD_DIM
DECAY_LORA = 64
ICLR_LORA = 64
GATE_LORA = 128
GN_EPS = 64e-5
L2_EPS = 1e-12
MLA_HEADS = 8
MLA_NOPE_DIM = 64
MLA_ROPE_DIM = 32
MLA_V_DIM = 64
D_MLA = MLA_HEADS * MLA_V_DIM
Q_LORA_RANK = 768
KV_LORA_RANK = 256
ROPE_THETA = 10000.0
MLA_SCALE = (MLA_NOPE_DIM + MLA_ROPE_DIM) ** -0.5
RWKV_COLS = 3 * D_RWKV + 2 * DECAY_LORA + 2 * ICLR_LORA + GATE_LORA
MLA_COLS_PAD = 1152
D_FF = 2816
NORM_EPS = 1e-6
QK_PAD = 128

VMEM_LIMIT = 56 * 1024 * 1024
BF16_SUBLANES = 16

WKV_CHUNK = 64
DECAY_LOG_SCALE = -math.exp(-0.5)

TILE_INPROJ = 512
TILE_RWKV_PREP = 256
TILE_WKV = 256
WKV_HEADS_PER_STEP = 8
TILE_MLA_PREP = 512
TILE_ATTN_Q = 256
TILE_MIX_OUT = 512
TILE_FFN = 512
FFN_COL_CHUNK = 256
FFN_HALO = BF16_SUBLANES


def _dot(a, b):
    return jnp.dot(a.astype(BF16), b.astype(BF16), preferred_element_type=F32)


def _dot_nt(a, b):
    return lax.dot_general(a.astype(BF16), b.astype(BF16), (((1,), (1,)), ((), ())),
                           preferred_element_type=F32)


def _dot_hilo(a, b):
    hi = a.astype(BF16)
    lo = (a - hi.astype(F32)).astype(BF16)
    return (jnp.dot(hi, b, preferred_element_type=F32)
            + jnp.dot(lo, b, preferred_element_type=F32))


def _sigmoid(x):
    return 1.0 / (1.0 + jnp.exp(-x))


def _rms(x, g):
    return x * lax.rsqrt(jnp.mean(x * x, axis=-1, keepdims=True) + NORM_EPS) * g


def _inproj_kernel(x_ref, g_ref, wr_ref, wm_ref, zr_ref, zm_ref):
    n = _rms(x_ref[...], g_ref[...]).astype(BF16)
    zr_ref[...] = jnp.dot(n, wr_ref[...], preferred_element_type=F32).astype(zr_ref.dtype)
    zm_ref[...] = jnp.dot(n, wm_ref[...], preferred_element_type=F32).astype(zm_ref.dtype)


def _inproj(x2, g, w_r, w_m):
    n_tok = x2.shape[0]
    tm = min(TILE_INPROJ, n_tok)
    full = lambda a: pl.BlockSpec(a.shape, lambda i: (0,) * a.ndim)
    return pl.pallas_call(
        _inproj_kernel,
        grid=(n_tok // tm,),
        in_specs=[pl.BlockSpec((tm, D_MODEL), lambda i: (i, 0)), full(g), full(w_r), full(w_m)],
        out_specs=[pl.BlockSpec((tm, RWKV_COLS), lambda i: (i, 0)),
                   pl.BlockSpec((tm, MLA_COLS_PAD), lambda i: (i, 0))],
        out_shape=[jax.ShapeDtypeStruct((n_tok, RWKV_COLS), BF16),
                   jax.ShapeDtypeStruct((n_tok, MLA_COLS_PAD), BF16)],
        compiler_params=pltpu.CompilerParams(dimension_semantics=("parallel",),
                                             vmem_limit_bytes=VMEM_LIMIT),
        name="inproj",
    )(x2, g, w_r, w_m)


def _rwkv_prep_kernel(z_ref, zp_ref, zn_ref, mup_ref, mun_ref, w0_ref, w2_ref, a0_ref, a2_ref,
                      g2_ref, kk_ref, ka_ref, rk_ref, ones_ref, eye_ref,
                      r_out, k_out, v_out, vt_out, kkn_out, af_out, ab_out, lwf_out, lwb_out,
                      bonus_out, g_out):
    i = pl.program_id(1)
    nt = pl.num_programs(1)
    z = z_ref[0].astype(F32)
    tt = z.shape[0]
    row = lax.broadcasted_iota(jnp.int32, (tt, 1), 0)
    has_prev = (i > 0).astype(F32)
    has_next = (i < nt - 1).astype(F32)
    prev_row = zp_ref[0][BF16_SUBLANES - 1:BF16_SUBLANES, :].astype(F32) * has_prev
    next_row = zn_ref[0][0:1, :].astype(F32) * has_next
    z_prev = jnp.where(row == 0, prev_row, pltpu.roll(z, 1, 0))
    z_next = jnp.where(row == tt - 1, next_row, pltpu.roll(z, tt - 1, 0))
    zs = z + mup_ref[...] * (z_prev - z) + mun_ref[...] * (z_next - z)

    d = D_RWKV
    r = zs[:, 0:d]
    k = zs[:, d:2 * d]
    v = zs[:, 2 * d:3 * d]
    wd = zs[:, 3 * d:3 * d + 128]
    ad = zs[:, 3 * d + 128:3 * d + 256]
    gd = zs[:, 3 * d + 256:3 * d + 384]

    logit = w0_ref[...] + _dot(jnp.tanh(wd), w2_ref[...])
    logw = DECAY_LOG_SCALE * _sigmoid(logit)
    a = _sigmoid(a0_ref[...] + _dot(ad, a2_ref[...]))
    g = _dot(_sigmoid(gd), g2_ref[...])
    a_f = a[:, :d]
    a_b = a[:, d:]

    ones_bd = ones_ref[...]
    kk = k * kk_ref[...]
    ss = _dot_hilo(kk * kk, ones_bd)
    kk = kk / jnp.maximum(jnp.sqrt(ss), L2_EPS)
    k_a = ka_ref[...]
    k_fb = k * (2.0 + (a_f + a_b - 2.0) * k_a)
    bonus = _dot_hilo(r * k_fb * rk_ref[...], ones_bd) * v

    v16 = v.astype(BF16)
    r_out[0] = r.astype(r_out.dtype)
    k_out[0] = k.astype(k_out.dtype)
    v_out[0] = v16
    zrows = jnp.zeros((WKV_CHUNK, d), BF16)
    for c in range(tt // WKV_CHUNK):
        vt_out[0, c] = lax.dot_general(
            eye_ref[...], jnp.concatenate([zrows, v16[c * WKV_CHUNK:(c + 1) * WKV_CHUNK, :]], axis=0),
            (((1,), (1,)), ((), ())), preferred_element_type=F32).astype(vt_out.dtype)
    kkn_out[0] = kk.astype(kkn_out.dtype)
    af_out[0] = a_f.astype(af_out.dtype)
    ab_out[0] = a_b.astype(ab_out.dtype)
    lwf_out[0] = logw[:, :d]
    lwb_out[0] = logw[:, d:]
    bonus_out[0] = bonus.astype(bonus_out.dtype)
    g_out[0] = g.astype(g_out.dtype)


def _rwkv_prep(z_r, params):
    B, T, _ = z_r.shape
    tt = min(TILE_RWKV_PREP, T)
    nt = T // tt
    hr = BF16_SUBLANES
    hb = tt // hr
    full = lambda a: pl.BlockSpec(a.shape, lambda b, i: (0,) * a.ndim)
    seq = lambda w: pl.BlockSpec((1, tt, w), lambda b, i: (b, i, 0))
    in_specs = [
        seq(RWKV_COLS),
        pl.BlockSpec((1, hr, RWKV_COLS), lambda b, i: (b, jnp.maximum(i * hb - 1, 0), 0)),
        pl.BlockSpec((1, hr, RWKV_COLS), lambda b, i: (b, jnp.minimum((i + 1) * hb, T // hr - 1), 0)),
    ] + [full(p) for p in params]
    seq16 = jax.ShapeDtypeStruct((B, T, D_RWKV), BF16)
    seq32 = jax.ShapeDtypeStruct((B, T, D_RWKV), F32)
    cpt = tt // WKV_CHUNK
    vt_spec = pl.BlockSpec((1, cpt, D_RWKV, 2 * WKV_CHUNK), lambda b, i: (b, i, 0, 0))
    vt_shape = jax.ShapeDtypeStruct((B, T // WKV_CHUNK, D_RWKV, 2 * WKV_CHUNK), BF16)
    return pl.pallas_call(
        _rwkv_prep_kernel,
        grid=(B, nt),
        in_specs=in_specs,
        out_specs=[seq(D_RWKV)] * 3 + [vt_spec] + [seq(D_RWKV)] * 7,
        out_shape=[seq16, seq16, seq16, vt_shape, seq16, seq16, seq16, seq32, seq32, seq16, seq16],
        compiler_params=pltpu.CompilerParams(dimension_semantics=("parallel", "parallel"),
                                             vmem_limit_bytes=VMEM_LIMIT),
        name="rwkv_prep",
    )(z_r, z_r, z_r, *params)


def _wkv_operands(rows, chunk, reverse, r_ref, k_ref, v_ref, vt_ref, kk_ref, a_ref, lw_ref, ka_ref):
    C = WKV_CHUNK
    lw = lw_ref[0, rows, :]
    ti = lax.broadcasted_iota(jnp.int32, (C, C), 0)
    si = lax.broadcasted_iota(jnp.int32, (C, C), 1)
    if reverse:
        incl = si >= ti
        strict = si > ti
    else:
        incl = si <= ti
        strict = si < ti
    tri = incl.astype(BF16)
    lw_hi = lw.astype(BF16)
    lw_lo = (lw - lw_hi.astype(F32)).astype(BF16)
    cum = (jnp.dot(tri, lw_hi, preferred_element_type=F32)
           + jnp.dot(tri, lw_lo, preferred_element_type=F32))
    e_inc = jnp.exp(cum)
    e_dec = jnp.exp(-cum)
    e_exc = jnp.exp(cum - lw)
    r = r_ref[0, rows, :].astype(F32)
    k = k_ref[0, rows, :].astype(F32)
    kk = kk_ref[0, rows, :].astype(F32)
    a = a_ref[0, rows, :].astype(F32)
    k_dir = k * (1.0 + (a - 1.0) * ka_ref[...])
    t2 = lax.broadcasted_iota(jnp.int32, (C, 2 * C), 0)
    l2 = lax.broadcasted_iota(jnp.int32, (C, 2 * C), 1)
    s2 = jnp.bitwise_and(l2, C - 1)
    hi = l2 >= C
    if reverse:
        incl2 = s2 >= t2
        strict2 = s2 > t2
    else:
        incl2 = s2 <= t2
        strict2 = s2 < t2
    return dict(
        incl2=incl2, strict_lo=jnp.logical_and(strict2, jnp.logical_not(hi)),
        strict_hi=jnp.logical_and(strict2, hi), sign2=jnp.where(hi, 1.0, -1.0).astype(F32),
        eye=(ti == si).astype(F32),
        p_end=jnp.exp(jnp.sum(lw, axis=0, keepdims=True)),
        alpha=(e_exc * kk).astype(BF16), beta=(e_dec * kk * a).astype(BF16),
        kappa=(e_dec * k_dir).astype(BF16), rho=(e_inc * r).astype(BF16),
        v=v_ref[0, rows, :], vt=vt_ref[0, chunk])


def _wkv_step(dirs, n_heads):
    C = WKV_CHUNK
    N = HEAD_DIM
    cat = jnp.concatenate
    zpad = jnp.zeros((C, C), BF16)
    widen = lambda a: cat([a, zpad], axis=1)
    chains = []
    for d in dirs:
        ops = d["ops"]
        for h in range(n_heads):
            sl = slice(h * N, (h + 1) * N)
            chains.append(dict(d=d, h=h, sl=sl, al=ops["alpha"][:, sl], be=ops["beta"][:, sl],
                               ka=ops["kappa"][:, sl], rh=ops["rho"][:, sl], vh=ops["v"][:, sl],
                               vt=ops["vt"][sl, :]))
    for c in chains:
        c["gram"] = _dot_nt(cat([c["al"], c["rh"]], axis=0), cat([c["be"], c["ka"]], axis=0))
    for c in chains:
        ops, g = c["d"]["ops"], c["gram"]
        c["m"] = jnp.where(ops["strict_lo"], -g[:C], 0.0)[:, :C].astype(BF16)
        ak_hi = jnp.where(ops["strict_hi"], g[:C], 0.0).astype(BF16)
        c["wt_rhs"] = widen(c["al"]) + ak_hi
        c["rbk"] = (jnp.where(ops["incl2"], g[C:], 0.0) * ops["sign2"]).astype(BF16)
        c["p"] = ops["eye"] + c["m"].astype(F32)
    for c in chains:
        c["mp"] = jnp.dot(c["m"], c["m"], preferred_element_type=F32)
    n_sq = int(math.log2(C)) - 1
    for s in range(n_sq):
        last = s == n_sq - 1
        for c in chains:
            mp = c["mp"].astype(BF16)
            p16 = c["p"].astype(BF16)
            out = jnp.dot(p16 if last else cat([p16, mp], axis=0), mp, preferred_element_type=F32)
            c["p"] = c["p"] + out[:C]
            if not last:
                c["mp"] = out[C:]
    for c in chains:
        c["wt"] = _dot(c["p"], c["wt_rhs"]).astype(BF16)
    for c in chains:
        c["S"] = c["d"]["s_ref"][c["h"]]
        c["sv"] = widen(c["S"].astype(BF16)) + c["vt"]
        c["x"] = _dot_nt(cat([c["wt"], widen(c["rh"])], axis=0), c["sv"])
        c["ut"] = _dot_nt(c["sv"], c["wt"])
    for c in chains:
        u = c["x"][:C].astype(BF16)
        c["y"] = c["x"][C:] + jnp.dot(c["rbk"], cat([u, c["vh"]], axis=0), preferred_element_type=F32)
        c["ds"] = jnp.dot(c["vt"] - widen(c["ut"].astype(BF16)), cat([c["be"], c["ka"]], axis=0),
                          preferred_element_type=F32)
    for c in chains:
        c["d"]["s_ref"][c["h"]] = (c["S"] + c["ds"]) * c["d"]["ops"]["p_end"][:, c["sl"]]
    for d in dirs:
        d["y_ref"][0, d["rows"], :] = cat([c["y"] for c in chains if c["d"] is d], axis=1).astype(d["y_ref"].dtype)


def _wkv_kernel(rf_ref, kf_ref, vf_ref, vtf_ref, kkf_ref, af_ref, lwf_ref,
                rb_ref, kb_ref, vb_ref, vtb_ref, kkb_ref, ab_ref, lwb_ref, ka_ref,
                yf_ref, yb_ref, sf_ref, sb_ref, *, n_heads, n_chunks):
    @pl.when(pl.program_id(2) == 0)
    def _():
        sf_ref[...] = jnp.zeros_like(sf_ref)
        sb_ref[...] = jnp.zeros_like(sb_ref)

    C = WKV_CHUNK

    def body(j, carry):
        jb = n_chunks - 1 - j
        rows_f = pl.ds(pl.multiple_of(j * C, C), C)
        rows_b = pl.ds(pl.multiple_of(jb * C, C), C)
        fwd = dict(rows=rows_f, y_ref=yf_ref, s_ref=sf_ref,
                   ops=_wkv_operands(rows_f, j, False, rf_ref, kf_ref, vf_ref, vtf_ref, kkf_ref, af_ref,
                                     lwf_ref, ka_ref))
        bwd = dict(rows=rows_b, y_ref=yb_ref, s_ref=sb_ref,
                   ops=_wkv_operands(rows_b, jb, True, rb_ref, kb_ref, vb_ref, vtb_ref, kkb_ref, ab_ref,
                                     lwb_ref, ka_ref))
        _wkv_step([fwd, bwd], n_heads)
        return carry

    lax.fori_loop(0, n_chunks, body, 0)


def _wkv(r, k, v, vt, kk, a_f, a_b, lw_f, lw_b, k_a):
    B, T, _ = r.shape
    tt = min(TILE_WKV, T)
    hg = WKV_HEADS_PER_STEP
    nt = T // tt
    ng = RWKV_HEADS // hg
    w = hg * HEAD_DIM
    cpt = tt // WKV_CHUNK
    fwd = pl.BlockSpec((1, tt, w), lambda b, g, i: (b, i, g))
    bwd = pl.BlockSpec((1, tt, w), lambda b, g, i: (b, nt - 1 - i, g))
    vt_fwd = pl.BlockSpec((1, cpt, w, 2 * WKV_CHUNK), lambda b, g, i: (b, i, g, 0))
    vt_bwd = pl.BlockSpec((1, cpt, w, 2 * WKV_CHUNK), lambda b, g, i: (b, nt - 1 - i, g, 0))
    ka_spec = pl.BlockSpec((1, w), lambda b, g, i: (0, g))
    kern = functools.partial(_wkv_kernel, n_heads=hg, n_chunks=cpt)
    return pl.pallas_call(
        kern,
        grid=(B, ng, nt),
        in_specs=[fwd] * 3 + [vt_fwd] + [fwd] * 3 + [bwd] * 3 + [vt_bwd] + [bwd] * 3 + [ka_spec],
        out_specs=[fwd, bwd],
        out_shape=[jax.ShapeDtypeStruct((B, T, D_RWKV), BF16)] * 2,
        scratch_shapes=[pltpu.VMEM((hg, HEAD_DIM, HEAD_DIM), F32)] * 2,
        compiler_params=pltpu.CompilerParams(
            dimension_semantics=("parallel", "parallel", "arbitrary"),
            vmem_limit_bytes=VMEM_LIMIT),
        name="wkv",
    )(r, k, v, vt, kk, a_f, lw_f, r, k, v, vt, kk, a_b, lw_b, k_a)


def _mla_prep_kernel(z_ref, qg_ref, kvg_ref, wq_ref, wqr_ref, wk_ref, wvt_ref, e2_ref,
                     cosq_ref, sinq_ref, cs_ref, q_out, k_out, vt_out):
    z = z_ref[0].astype(F32)
    c_q = z[:, :Q_LORA_RANK]
    c_kv = z[:, Q_LORA_RANK:Q_LORA_RANK + KV_LORA_RANK]
    kr = z[:, Q_LORA_RANK + KV_LORA_RANK:]
    nq = _rms(c_q, qg_ref[...]).astype(BF16)
    nkv = _rms(c_kv, kvg_ref[...]).astype(BF16)
    q = (jnp.dot(nq, wq_ref[...], preferred_element_type=F32) * cosq_ref[...]
         + jnp.dot(nq, wqr_ref[...], preferred_element_type=F32) * sinq_ref[...])
    kf = (jnp.dot(nkv, wk_ref[...], preferred_element_type=F32)
          + _dot(kr * cs_ref[...], e2_ref[...]))
    vt = lax.dot_general(wvt_ref[...], nkv, (((1,), (1,)), ((), ())), preferred_element_type=F32)
    for h in range(MLA_HEADS):
        q_out[0, h] = q[:, h * QK_PAD:(h + 1) * QK_PAD].astype(q_out.dtype)
        k_out[0, h] = kf[:, h * QK_PAD:(h + 1) * QK_PAD].astype(k_out.dtype)
    vt_out[0] = vt.astype(vt_out.dtype)


def _mla_prep(z_m, params, tabs):
    B, T, _ = z_m.shape
    tt = min(TILE_MLA_PREP, T)
    full = lambda a: pl.BlockSpec(a.shape, lambda b, i: (0,) * a.ndim)
    tab = lambda a: pl.BlockSpec((tt, a.shape[1]), lambda b, i: (i, 0))
    hspec = pl.BlockSpec((1, MLA_HEADS, tt, QK_PAD), lambda b, i: (b, 0, i, 0))
    return pl.pallas_call(
        _mla_prep_kernel,
        grid=(B, T // tt),
        in_specs=[pl.BlockSpec((1, tt, MLA_COLS_PAD), lambda b, i: (b, i, 0))]
        + [full(p) for p in params] + [tab(t) for t in tabs],
        out_specs=[hspec, hspec, pl.BlockSpec((1, D_MLA, tt), lambda b, i: (b, 0, i))],
        out_shape=[jax.ShapeDtypeStruct((B, MLA_HEADS, T, QK_PAD), BF16),
                   jax.ShapeDtypeStruct((B, MLA_HEADS, T, QK_PAD), BF16),
                   jax.ShapeDtypeStruct((B, D_MLA, T), BF16)],
        compiler_params=pltpu.CompilerParams(dimension_semantics=("parallel", "parallel"),
                                             vmem_limit_bytes=VMEM_LIMIT),
        name="mla_prep",
    )(z_m, *params, *tabs)


def _attn_kernel(q_ref, k_ref, vt_ref, og_ref, o_ref):
    def scores(h):
        return _dot_nt(k_ref[0, h], q_ref[0, h])

    outs = []
    s_next = scores(0)
    for h in range(MLA_HEADS):
        s = s_next
        if h + 1 < MLA_HEADS:
            s_next = scores(h + 1)
        m = jnp.max(s, axis=0, keepdims=True)
        p = jnp.exp2(s - m)
        l = jnp.sum(p, axis=0, keepdims=True)
        o = jnp.dot(vt_ref[0, h * MLA_V_DIM:(h + 1) * MLA_V_DIM, :], p.astype(BF16),
                    preferred_element_type=F32)
        outs.append(o / l)
    o = jnp.concatenate(outs, axis=0)
    y = o * lax.rsqrt(jnp.mean(o * o, axis=0, keepdims=True) + NORM_EPS) * og_ref[...]
    o_ref[0] = y.T.astype(o_ref.dtype)


def _attn(q, k, vt, out_g_col):
    B, H, T, _ = q.shape
    tq = min(TILE_ATTN_Q, T)
    return pl.pallas_call(
        _attn_kernel,
        grid=(B, T // tq),
        in_specs=[pl.BlockSpec((1, H, tq, QK_PAD), lambda b, i: (b, 0, i, 0)),
                  pl.BlockSpec((1, H, T, QK_PAD), lambda b, i: (b, 0, 0, 0)),
                  pl.BlockSpec((1, D_MLA, T), lambda b, i: (b, 0, 0)),
                  pl.BlockSpec(out_g_col.shape, lambda b, i: (0, 0))],
        out_specs=pl.BlockSpec((1, tq, D_MLA), lambda b, i: (b, i, 0)),
        out_shape=jax.ShapeDtypeStruct((B, T, D_MLA), BF16),
        compiler_params=pltpu.CompilerParams(dimension_semantics=("parallel", "arbitrary"),
                                             vmem_limit_bytes=VMEM_LIMIT),
        name="attn",
    )(q, k, vt, out_g_col)


def _mix_out_kernel(yf_ref, yb_ref, bonus_ref, g_ref, ym_ref, x_ref, lng_ref, lnb_ref, ones_ref,
                    wo_r_ref, wo_m_ref, fg_ref, h_out, n_out):
    y = yf_ref[...].astype(F32) + yb_ref[...].astype(F32)
    ones_bd = ones_ref[...]
    inv_n = 1.0 / HEAD_DIM
    mu = _dot_hilo(y, ones_bd) * inv_n
    yc = y - mu
    var = _dot_hilo(yc * yc, ones_bd) * inv_n
    gn = yc * lax.rsqrt(var + GN_EPS) * lng_ref[...] + lnb_ref[...]
    y_rwkv = (gn + bonus_ref[...].astype(F32)) * g_ref[...].astype(F32)
    h = (x_ref[...] + _dot(y_rwkv, wo_r_ref[...])
         + jnp.dot(ym_ref[...], wo_m_ref[...], preferred_element_type=F32))
    h_out[...] = h
    n_out[...] = _rms(h, fg_ref[...]).astype(n_out.dtype)


def _mix_out(y_f, y_b, bonus, g, y_mla, x2, params):
    n_tok = x2.shape[0]
    tm = min(TILE_MIX_OUT, n_tok)
    full = lambda a: pl.BlockSpec(a.shape, lambda i: (0,) * a.ndim)
    tok = lambda w: pl.BlockSpec((tm, w), lambda i: (i, 0))
    return pl.pallas_call(
        _mix_out_kernel,
        grid=(n_tok // tm,),
        in_specs=[tok(D_RWKV)] * 4 + [tok(D_MLA), tok(D_MODEL)] + [full(p) for p in params],
        out_specs=[tok(D_MODEL), tok(D_MODEL)],
        out_shape=[jax.ShapeDtypeStruct((n_tok, D_MODEL), F32),
                   jax.ShapeDtypeStruct((n_tok, D_MODEL), BF16)],
        compiler_params=pltpu.CompilerParams(dimension_semantics=("parallel",),
                                             vmem_limit_bytes=VMEM_LIMIT),
        name="mix_out",
    )(y_f, y_b, bonus, g, y_mla, x2, *params)


def _ffn_kernel(n_ref, h_ref, wg_ref, wv_ref, cg_ref, cv_ref, bg_ref, bv_ref, wd_ref, fg_ref,
                o_ref, npad_ref, act_ref):
    i = pl.program_id(1)
    T = n_ref.shape[1]
    R = o_ref.shape[1]
    H = FFN_HALO

    @pl.when(i == 0)
    def _():
        zeros = jnp.zeros((H, D_MODEL), npad_ref.dtype)
        npad_ref[0:H, :] = zeros
        npad_ref[H + T:H + T + H, :] = zeros
        npad_ref[H:H + T, :] = n_ref[0]

    n_win = npad_ref[pl.ds(pl.multiple_of(i * R, R), R + 2 * H), :]

    def conv(u, c, b):
        return c[0:1] * u[H - 1:H - 1 + R] + c[1:2] * u[H:H + R] + c[2:3] * u[H + 1:H + 1 + R] + b

    fc = FFN_COL_CHUNK
    for j in range(D_FF // fc):
        cs = slice(j * fc, (j + 1) * fc)
        gate = conv(jnp.dot(n_win, wg_ref[:, cs], preferred_element_type=F32), cg_ref[:, cs], bg_ref[:, cs])
        val = conv(jnp.dot(n_win, wv_ref[:, cs], preferred_element_type=F32), cv_ref[:, cs], bv_ref[:, cs])
        act_ref[:, cs] = (gate * _sigmoid(gate) * val).astype(act_ref.dtype)
    f = jnp.dot(act_ref[...], wd_ref[...], preferred_element_type=F32)
    o_ref[0] = _rms(h_ref[0] + f, fg_ref[...])


def _ffn(n2, h1, w_gate, w_val, c_gate, c_val, b_gate, b_val, w_down, final_g):
    B, T, _ = n2.shape
    R = min(TILE_FFN, T)
    once = lambda a: pl.BlockSpec(a.shape, lambda b, i: (0,) * a.ndim, pipeline_mode=pl.Buffered(1))
    rows = pl.BlockSpec((1, R, D_MODEL), lambda b, i: (b, i, 0))
    return pl.pallas_call(
        _ffn_kernel,
        grid=(B, T // R),
        in_specs=[pl.BlockSpec((1, T, D_MODEL), lambda b, i: (b, 0, 0)), rows]
        + [once(a) for a in (w_gate, w_val, c_gate, c_val, b_gate, b_val, w_down, final_g)],
        out_specs=rows,
        out_shape=jax.ShapeDtypeStruct((B, T, D_MODEL), F32),
        scratch_shapes=[pltpu.VMEM((T + 2 * FFN_HALO, D_MODEL), BF16),
                        pltpu.VMEM((R, D_FF), BF16)],
        compiler_params=pltpu.CompilerParams(dimension_semantics=("parallel", "arbitrary"),
                                             vmem_limit_bytes=VMEM_LIMIT),
        name="ffn",
    )(n2, h1, w_gate, w_val, c_gate, c_val, b_gate, b_val, w_down, final_g)


def _rot_cols(w):
    half = w.shape[-1] // 2
    return jnp.concatenate([-w[..., half:], w[..., :half]], axis=-1)


def _block_diag2(a, b):
    za = jnp.zeros((a.shape[0], b.shape[1]), a.dtype)
    zb = jnp.zeros((b.shape[0], a.shape[1]), b.dtype)
    return jnp.concatenate([jnp.concatenate([a, za], axis=1), jnp.concatenate([zb, b], axis=1)], axis=0)


def _head_ones():
    idx = np.arange(D_RWKV) // HEAD_DIM
    return jnp.asarray((idx[:, None] == idx[None, :]).astype(np.float32), dtype=BF16)


def _rope_tables(T):
    inv_freq = jnp.power(ROPE_THETA, -jnp.arange(0, MLA_ROPE_DIM, 2, dtype=F32) / MLA_ROPE_DIM)
    ang = jnp.arange(T, dtype=F32)[:, None] * inv_freq[None, :]
    ang = jnp.concatenate([ang, ang], axis=-1)
    cos, sin = jnp.cos(ang), jnp.sin(ang)
    one = jnp.ones((T, MLA_NOPE_DIM), F32)
    zn = jnp.zeros((T, MLA_NOPE_DIM), F32)
    zp = jnp.zeros((T, QK_PAD - MLA_NOPE_DIM - MLA_ROPE_DIM), F32)
    q_scale = MLA_SCALE * math.log2(math.e)
    cosq = jnp.tile(jnp.concatenate([one, cos, zp], axis=1), (1, MLA_HEADS)) * q_scale
    sinq = jnp.tile(jnp.concatenate([zn, sin, zp], axis=1), (1, MLA_HEADS)) * q_scale
    cs = jnp.concatenate([cos, sin, jnp.zeros((T, 128 - 2 * MLA_ROPE_DIM), F32)], axis=1)
    return cosq, sinq, cs


def _pad_heads(w, src_width, src_off, n, dst_off):
    K = w.shape[0]
    w = w.reshape(K, MLA_HEADS, src_width)[:, :, src_off:src_off + n]
    out = jnp.zeros((K, MLA_HEADS, QK_PAD), w.dtype)
    out = out.at[:, :, dst_off:dst_off + n].set(w)
    return out.reshape(K, MLA_HEADS * QK_PAD)


def kernel(x, ln_mix_g, w_in, shift_mu_prev, shift_mu_next, decay_w0_fwd, decay_w2_fwd, decay_w0_bwd, decay_w2_bwd, iclr_a0_fwd, iclr_a2_fwd, iclr_a0_bwd, iclr_a2_bwd, gate_g2, k_k, k_a, r_k, ln_x_g, ln_x_b, q_norm_g, w_uq, kv_norm_g, w_ukv, mla_out_g, w_out, ln_ffn_g, w_ffn_up, ffn_conv_w, ffn_conv_b, w_ffn_down, ln_final_g):
    B, T, D = x.shape
    assert w_in.shape[0] == 1, "single-layer block"
    n_tok = B * T
    row = lambda a: a.reshape(1, -1).astype(F32)
    l = 0

    w_in_l = w_in[l]
    w_r = w_in_l[:, :RWKV_COLS].astype(BF16)
    w_kr = w_in_l[:, RWKV_COLS + Q_LORA_RANK + KV_LORA_RANK:]
    w_m = jnp.concatenate([w_in_l[:, RWKV_COLS:], _rot_cols(w_kr),
                           jnp.zeros((D, MLA_COLS_PAD - (Q_LORA_RANK + KV_LORA_RANK + 2 * MLA_ROPE_DIM)), F32)],
                          axis=1).astype(BF16)
    ones_bd = _head_ones()
    prep_params = [
        row(shift_mu_prev[l]), row(shift_mu_next[l]),
        row(jnp.concatenate([decay_w0_fwd[l], decay_w0_bwd[l]])),
        _block_diag2(decay_w2_fwd[l], decay_w2_bwd[l]).astype(BF16),
        row(jnp.concatenate([iclr_a0_fwd[l], iclr_a0_bwd[l]])),
        _block_diag2(iclr_a2_fwd[l], iclr_a2_bwd[l]).astype(BF16),
        gate_g2[l].astype(BF16),
        row(k_k[l]), row(k_a[l]), row(r_k[l]), ones_bd, jnp.eye(D_RWKV, dtype=BF16),
    ]
    qw = MLA_NOPE_DIM + MLA_ROPE_DIM
    w_uq_l = w_uq[l]
    wq_pad = (_pad_heads(w_uq_l, qw, 0, MLA_NOPE_DIM, 0)
              + _pad_heads(w_uq_l, qw, MLA_NOPE_DIM, MLA_ROPE_DIM, MLA_NOPE_DIM)).astype(BF16)
    rope_cols = w_uq_l.reshape(Q_LORA_RANK, MLA_HEADS, qw)[:, :, MLA_NOPE_DIM:]
    rot_full = jnp.concatenate([jnp.zeros((Q_LORA_RANK, MLA_HEADS, MLA_NOPE_DIM), F32), _rot_cols(rope_cols)],
                               axis=-1).reshape(Q_LORA_RANK, MLA_HEADS * qw)
    wq_rot_pad = _pad_heads(rot_full, qw, MLA_NOPE_DIM, MLA_ROPE_DIM, MLA_NOPE_DIM).astype(BF16)
    kvw = MLA_NOPE_DIM + MLA_V_DIM
    w_ukv_l = w_ukv[l]
    wk_pad = _pad_heads(w_ukv_l, kvw, 0, MLA_NOPE_DIM, 0).astype(BF16)
    wv_t = w_ukv_l.reshape(KV_LORA_RANK, MLA_HEADS, kvw)[:, :, MLA_NOPE_DIM:].reshape(KV_LORA_RANK, D_MLA).T.astype(BF16)
    e2 = np.zeros((128, MLA_HEADS * QK_PAD), np.float32)
    for h in range(MLA_HEADS):
        for j in range(MLA_ROPE_DIM):
            e2[j, h * QK_PAD + MLA_NOPE_DIM + j] = 1.0
            e2[MLA_ROPE_DIM + j, h * QK_PAD + MLA_NOPE_DIM + j] = 1.0
    e2 = jnp.asarray(e2, dtype=BF16)
    cosq, sinq, cs = _rope_tables(T)
    mla_params = [row(q_norm_g[l]), row(kv_norm_g[l]), wq_pad, wq_rot_pad, wk_pad, wv_t, e2]
    w_out_l = w_out[l]
    mix_params = [row(ln_x_g[l]), row(ln_x_b[l]), ones_bd,
                  w_out_l[:D_RWKV].astype(BF16), w_out_l[D_RWKV:].astype(BF16), row(ln_ffn_g[l])]
    w_up = w_ffn_up[l]
    conv_w = ffn_conv_w[l]
    conv_b = ffn_conv_b[l]

    x2 = x.reshape(n_tok, D)
    z_r, z_m = _inproj(x2, row(ln_mix_g[l]), w_r, w_m)
    z_r = z_r.reshape(B, T, RWKV_COLS)
    z_m = z_m.reshape(B, T, MLA_COLS_PAD)

    r, k, v, vt, kk, a_f, a_b, lw_f, lw_b, bonus, g = _rwkv_prep(z_r, prep_params)
    y_f, y_b = _wkv(r, k, v, vt, kk, a_f, a_b, lw_f, lw_b, row(k_a[l]))

    q, kf, vt_mla = _mla_prep(z_m, mla_params, [cosq, sinq, cs])
    y_mla = _attn(q, kf, vt_mla, mla_out_g[l].reshape(D_MLA, 1).astype(F32))

    flat = lambda a: a.reshape(n_tok, a.shape[-1])
    h1, n2 = _mix_out(flat(y_f), flat(y_b), flat(bonus), flat(g), flat(y_mla), x2, mix_params)

    return _ffn(n2.reshape(B, T, D), h1.reshape(B, T, D),
                w_up[:, :D_FF].astype(BF16), w_up[:, D_FF:].astype(BF16),
                conv_w[:, :D_FF], conv_w[:, D_FF:], row(conv_b[:D_FF]), row(conv_b[D_FF:]),
                w_ffn_down[l].astype(BF16), row(ln_final_g))
```

```python
import functools
import math

import numpy as np
import jax
import jax.numpy as jnp
from jax import lax
from jax.experimental import pallas as pl
from jax.experimental.pallas import tpu as pltpu

F32 = jnp.float32
BF16 = jnp.bfloat16

D_MODEL = 1024
RWKV_HEADS = 8
HEAD_DIM = 64
D_RWKV = RWKV_HEADS * HEAD_DIM
DECAY_LORA = 64
ICLR_LORA = 64
GATE_LORA = 128
GN_EPS = 64e-5
L2_EPS = 1e-12
MLA_HEADS = 8
MLA_NOPE_DIM = 64
MLA_ROPE_DIM = 32
MLA_V_DIM = 64
D_MLA = MLA_HEADS * MLA_V_DIM
Q_LORA_RANK = 768
KV_LORA_RANK = 256
ROPE_THETA = 10000.0
MLA_SCALE = (MLA_NOPE_DIM + MLA_ROPE_DIM) ** -0.5
RWKV_COLS = 3 * D_RWKV + 2 * DECAY_LORA + 2 * ICLR_LORA + GATE_LORA
MLA_COLS_PAD = 1152
D_FF = 2816
NORM_EPS = 1e-6
QK_PAD = 128
VMEM_LIMIT = 56 * 1024 * 1024
BF16_SUBLANES = 16

WKV_CHUNK = 64
DECAY_LOG_SCALE = -math.exp(-0.5)

TILE_INPROJ = 512
TILE_RWKV_PREP = 256
TILE_WKV = 256
WKV_HEADS_PER_STEP = 8
WKV_BATCH_PER_STEP = 2
TILE_MLA_PREP = 512
TILE_ATTN_Q = 512
TILE_MIX_OUT = 512
TILE_FFN = 512
FFN_COL_CHUNK = 256
FFN_HALO = BF16_SUBLANES


def _dot(a, b):
    return jnp.dot(a.astype(BF16), b.astype(BF16), preferred_element_type=F32)


def _dot_nt(a, b):
    return lax.dot_general(a.astype(BF16), b.astype(BF16), (((1,), (1,)), ((), ())),
                           preferred_element_type=F32)


def _sigmoid(x):
    return 1.0 / (1.0 + jnp.exp(-x))


def _rms(x, g):
    return x * lax.rsqrt(jnp.mean(x * x, axis=-1, keepdims=True) + NORM_EPS) * g


def _inproj_kernel(x_ref, g_ref, wr_ref, wm_ref, zr_ref, zm_ref):
    n = _rms(x_ref[...], g_ref[...]).astype(BF16)
    zr_ref[...] = jnp.dot(n, wr_ref[...], preferred_element_type=F32).astype(zr_ref.dtype)
    zm_ref[...] = jnp.dot(n, wm_ref[...], preferred_element_type=F32).astype(zm_ref.dtype)


def _inproj(x2, g, w_r, w_m):
    n_tok = x2.shape[0]
    tm = min(TILE_INPROJ, n_tok)
    full = lambda a: pl.BlockSpec(a.shape, lambda i: (0,) * a.ndim)
    return pl.pallas_call(
        _inproj_kernel,
        grid=(n_tok // tm,),
        in_specs=[pl.BlockSpec((tm, D_MODEL), lambda i: (i, 0)), full(g), full(w_r), full(w_m)],
        out_specs=[pl.BlockSpec((tm, RWKV_COLS), lambda i: (i, 0)),
                   pl.BlockSpec((tm, MLA_COLS_PAD), lambda i: (i, 0))],
        out_shape=[jax.ShapeDtypeStruct((n_tok, RWKV_COLS), BF16),
                   jax.ShapeDtypeStruct((n_tok, MLA_COLS_PAD), BF16)],
        compiler_params=pltpu.CompilerParams(dimension_semantics=("parallel",),
                                             vmem_limit_bytes=VMEM_LIMIT),
        name="inproj",
    )(x2, g, w_r, w_m)


def _rwkv_prep_kernel(z_ref, zp_ref, zn_ref, mup_ref, mun_ref, w0_ref, w2_ref, a0_ref, a2_ref,
                      g2_ref, kk_ref, ka_ref, rk_ref, ones_ref, eye_ref,
                      r_out, k_out, v_out, vt_out, kkn_out, af_out, ab_out, lwf_out, lwb_out,
                      bonus_out, g_out):
    i = pl.program_id(1)
    nt = pl.num_programs(1)
    z = z_ref[0].astype(F32)
    tt = z.shape[0]
    row = lax.broadcasted_iota(jnp.int32, (tt, 1), 0)
    has_prev = (i > 0).astype(F32)
    has_next = (i < nt - 1).astype(F32)
    prev_row = zp_ref[0][BF16_SUBLANES - 1:BF16_SUBLANES, :].astype(F32) * has_prev
    next_row = zn_ref[0][0:1, :].astype(F32) * has_next
    z_prev = jnp.where(row == 0, prev_row, pltpu.roll(z, 1, 0))
    z_next = jnp.where(row == tt - 1, next_row, pltpu.roll(z, tt - 1, 0))
    zs = z + mup_ref[...] * (z_prev - z) + mun_ref[...] * (z_next - z)

    d = D_RWKV
    r = zs[:, 0:d]
    k = zs[:, d:2 * d]
    v = zs[:, 2 * d:3 * d]
    wd = zs[:, 3 * d:3 * d + 128]
    ad = zs[:, 3 * d + 128:3 * d + 256]
    gd = zs[:, 3 * d + 256:3 * d + 384]

    logit = w0_ref[...] + _dot(jnp.tanh(wd), w2_ref[...])
    logw = DECAY_LOG_SCALE * _sigmoid(logit)
    a = _sigmoid(a0_ref[...] + _dot(ad, a2_ref[...]))
    g = _dot(_sigmoid(gd), g2_ref[...])
    a_f = a[:, :d]
    a_b = a[:, d:]

    ones_bd = ones_ref[...]
    kk = k * kk_ref[...]
    ss = _dot(kk * kk, ones_bd)
    kk = kk / jnp.maximum(jnp.sqrt(ss), L2_EPS)
    k_a = ka_ref[...]
    k_fb = k * (2.0 + (a_f + a_b - 2.0) * k_a)
    bonus = _dot(r * k_fb * rk_ref[...], ones_bd) * v

    v16 = v.astype(BF16)
    r_out[0] = r.astype(r_out.dtype)
    k_out[0] = k.astype(k_out.dtype)
    v_out[0] = v16
    ch = lax.broadcasted_iota(jnp.int32, (d, 2 * WKV_CHUNK), 0)
    ln = lax.broadcasted_iota(jnp.int32, (d, 2 * WKV_CHUNK), 1)
    odd_head = jnp.bitwise_and(ch // HEAD_DIM, 1) == 1
    keep_vt = odd_head == (ln < WKV_CHUNK)
    for c in range(tt // WKV_CHUNK):
        vc = v16[c * WKV_CHUNK:(c + 1) * WKV_CHUNK, :]
        vt2 = lax.dot_general(eye_ref[...], jnp.concatenate([vc, vc], axis=0),
                              (((1,), (1,)), ((), ())), preferred_element_type=F32)
        vt_out[0, c] = jnp.where(keep_vt, vt2, 0.0).astype(vt_out.dtype)
    kkn_out[0] = kk.astype(kkn_out.dtype)
    af_out[0] = a_f.astype(af_out.dtype)
    ab_out[0] = a_b.astype(ab_out.dtype)
    lwf_out[0] = logw[:, :d]
    lwb_out[0] = logw[:, d:]
    bonus_out[0] = bonus.astype(bonus_out.dtype)
    g_out[0] = g.astype(g_out.dtype)


def _rwkv_prep(z_r, params):
    B, T, _ = z_r.shape
    tt = min(TILE_RWKV_PREP, T)
    nt = T // tt
    hr = BF16_SUBLANES
    hb = tt // hr
    full = lambda a: pl.BlockSpec(a.shape, lambda b, i: (0,) * a.ndim)
    seq = lambda w: pl.BlockSpec((1, tt, w), lambda b, i: (b, i, 0))
    in_specs = [
        seq(RWKV_COLS),
        pl.BlockSpec((1, hr, RWKV_COLS), lambda b, i: (b, jnp.maximum(i * hb - 1, 0), 0)),
        pl.BlockSpec((1, hr, RWKV_COLS), lambda b, i: (b, jnp.minimum((i + 1) * hb, T // hr - 1), 0)),
    ] + [full(p) for p in params]
    seq16 = jax.ShapeDtypeStruct((B, T, D_RWKV), BF16)
    seq32 = jax.ShapeDtypeStruct((B, T, D_RWKV), F32)
    cpt = tt // WKV_CHUNK
    vt_spec = pl.BlockSpec((1, cpt, D_RWKV, 2 * WKV_CHUNK), lambda b, i: (b, i, 0, 0))
    vt_shape = jax.ShapeDtypeStruct((B, T // WKV_CHUNK, D_RWKV, 2 * WKV_CHUNK), BF16)
    return pl.pallas_call(
        _rwkv_prep_kernel,
        grid=(B, nt),
        in_specs=in_specs,
        out_specs=[seq(D_RWKV)] * 3 + [vt_spec] + [seq(D_RWKV)] * 7,
        out_shape=[seq16, seq16, seq16, vt_shape, seq16, seq16, seq16, seq32, seq32, seq16, seq16],
        compiler_params=pltpu.CompilerParams(dimension_semantics=("parallel", "parallel"),
                                             vmem_limit_bytes=VMEM_LIMIT),
        name="rwkv_prep",
    )(z_r, z_r, z_r, *params)


def _wkv_operands(bi, rows, chunk, reverse, r_ref, k_ref, v_ref, vt_ref, kk_ref, a_ref, lw_ref, ka_ref):
    C = WKV_CHUNK
    lw = lw_ref[bi, rows, :]
    ti = lax.broadcasted_iota(jnp.int32, (C, C), 0)
    si = lax.broadcasted_iota(jnp.int32, (C, C), 1)
    if reverse:
        incl = si >= ti
        strict = si > ti
    else:
        incl = si <= ti
        strict = si < ti
    tri = incl.astype(BF16)
    lw_hi = lw.astype(BF16)
    lw_lo = (lw - lw_hi.astype(F32)).astype(BF16)
    cum = (jnp.dot(tri, lw_hi, preferred_element_type=F32)
           + jnp.dot(tri, lw_lo, preferred_element_type=F32))
    e_inc = jnp.exp(cum)
    e_dec = jnp.exp(-cum)
    e_exc = jnp.exp(cum - lw)
    r = r_ref[bi, rows, :].astype(F32)
    k = k_ref[bi, rows, :].astype(F32)
    kk = kk_ref[bi, rows, :].astype(F32)
    a = a_ref[bi, rows, :].astype(F32)
    k_dir = k * (1.0 + (a - 1.0) * ka_ref[...])
    t2 = lax.broadcasted_iota(jnp.int32, (C, 2 * C), 0)
    l2 = lax.broadcasted_iota(jnp.int32, (C, 2 * C), 1)
    s2 = jnp.bitwise_and(l2, C - 1)
    hi = l2 >= C
    if reverse:
        incl2 = s2 >= t2
        strict2 = s2 > t2
    else:
        incl2 = s2 <= t2
        strict2 = s2 < t2
    return dict(
        incl2=incl2, strict2=strict2, lo=jnp.logical_not(hi), hi=hi,
        sign_lo_neg=jnp.where(hi, 1.0, -1.0).astype(F32), eye2=(s2 == t2).astype(F32),
        p_end=jnp.exp(jnp.sum(lw, axis=0, keepdims=True)),
        alpha=(e_exc * kk).astype(BF16), beta=(e_dec * kk * a).astype(BF16),
        kappa=(e_dec * k_dir).astype(BF16), rho=(e_inc * r).astype(BF16),
        v=v_ref[bi, rows, :], vt=vt_ref[bi, chunk])


def _wkv_step(dirs, n_pairs):
    C = WKV_CHUNK
    W = 2 * C
    cat = jnp.concatenate
    zc = jnp.zeros((C, W), BF16)
    lo2 = lax.broadcasted_iota(jnp.int32, (W, W), 1) < C

    def keep(mask, a):
        return jnp.where(mask, a, jnp.zeros_like(a))

    def bd(a, lo, hi):
        return cat([keep(lo, a), keep(hi, a)], axis=0)

    def bd2(a_e, a_o):
        return cat([cat([a_e, zc], axis=1), cat([zc, a_o], axis=1)], axis=0)

    chains = []
    for d in dirs:
        ops = d["ops"]
        for pi in range(n_pairs):
            sl = slice(pi * W, (pi + 1) * W)
            chains.append(dict(d=d, pi=pi, sl=sl, al=ops["alpha"][:, sl], be=ops["beta"][:, sl],
                               ka=ops["kappa"][:, sl], rh=ops["rho"][:, sl], v=ops["v"][:, sl],
                               vt_e=ops["vt"][pi * W:pi * W + C, :],
                               vt_o=ops["vt"][pi * W + C:(pi + 1) * W, :]))
    for c in chains:
        ar = cat([c["al"], c["rh"]], axis=0)
        c["g_e"] = _dot_nt(keep(lo2, ar), cat([c["be"], c["ka"]], axis=0))
        c["g_o"] = _dot_nt(keep(jnp.logical_not(lo2), ar), cat([c["ka"], c["be"]], axis=0))
    for c in chains:
        ops = c["d"]["ops"]
        lo, hi, strict2, incl2 = ops["lo"], ops["hi"], ops["strict2"], ops["incl2"]
        ga_e, gr_e = c["g_e"][:C], c["g_e"][C:]
        ga_o, gr_o = c["g_o"][:C], c["g_o"][C:]
        c["m"] = jnp.where(strict2, jnp.where(lo, -ga_e, -ga_o), 0.0).astype(BF16)
        ak_e = jnp.where(strict2, ga_e, 0.0).astype(BF16)
        ak_o = jnp.where(strict2, ga_o, 0.0).astype(BF16)
        c["wt_rhs"] = bd2(jnp.where(lo, c["al"], ak_e), jnp.where(hi, c["al"], ak_o))
        rbk_e = jnp.where(incl2, gr_e, 0.0) * ops["sign_lo_neg"]
        rbk_o = jnp.where(incl2, gr_o, 0.0) * (-ops["sign_lo_neg"])
        c["rbk"] = cat([rbk_e, rbk_o], axis=1).astype(BF16)
        c["P"] = ops["eye2"] + c["m"].astype(F32)
    for c in chains:
        ops = c["d"]["ops"]
        c["mp"] = jnp.dot(c["m"], bd(c["m"], ops["lo"], ops["hi"]), preferred_element_type=F32)
    n_sq = int(math.log2(C)) - 1
    for s in range(n_sq):
        last = s == n_sq - 1
        for c in chains:
            ops = c["d"]["ops"]
            mp = c["mp"].astype(BF16)
            p16 = c["P"].astype(BF16)
            out = jnp.dot(p16 if last else cat([p16, mp], axis=0), bd(mp, ops["lo"], ops["hi"]),
                          preferred_element_type=F32)
            c["P"] = c["P"] + out[:C]
            if not last:
                c["mp"] = out[C:]
    for c in chains:
        wt = _dot(c["P"], c["wt_rhs"]).astype(BF16)
        c["wt_e"], c["wt_o"] = wt[:, :W], wt[:, W:]
    for c in chains:
        ops = c["d"]["ops"]
        lo, hi = ops["lo"], ops["hi"]
        c["S"] = c["d"]["s_ref"][c["d"]["bi"], c["pi"]]
        s16 = c["S"].astype(BF16)
        sv_e = keep(lo, s16) + c["vt_e"]
        sv_o = keep(hi, s16) + c["vt_o"]
        lhs = cat([cat([c["wt_e"], keep(lo, c["rh"])], axis=0),
                   cat([c["wt_o"], keep(hi, c["rh"])], axis=0)], axis=1)
        c["x"] = _dot_nt(lhs, bd2(sv_e, sv_o))
        c["ut"] = _dot_nt(cat([sv_e, sv_o], axis=1), bd2(c["wt_e"], c["wt_o"]))
    for c in chains:
        ops = c["d"]["ops"]
        lo, hi = ops["lo"], ops["hi"]
        u = c["x"][:C].astype(BF16)
        vu = cat([keep(lo, u), keep(lo, c["v"]), keep(hi, c["v"]), keep(hi, u)], axis=0)
        c["y"] = c["x"][C:] + jnp.dot(c["rbk"], vu, preferred_element_type=F32)
        ut = c["ut"].astype(BF16)
        lhs = cat([c["vt_e"] - keep(lo, ut), c["vt_o"] - keep(hi, ut)], axis=1)
        kb = cat([keep(lo, c["be"]), keep(lo, c["ka"]), keep(hi, c["ka"]), keep(hi, c["be"])], axis=0)
        c["ds"] = jnp.dot(lhs, kb, preferred_element_type=F32)
    for c in chains:
        c["d"]["s_ref"][c["d"]["bi"], c["pi"]] = (c["S"] + c["ds"]) * c["d"]["ops"]["p_end"][:, c["sl"]]
    for d in dirs:
        d["y_ref"][d["bi"], d["rows"], :] =cat([c["y"] for c in chains if c["d"] is d], axis=1).astype(d["y_ref"].dtype)


def _wkv_kernel(rf_ref, kf_ref, vf_ref, vtf_ref, kkf_ref, af_ref, lwf_ref,
                rb_ref, kb_ref, vb_ref, vtb_ref, kkb_ref, ab_ref, lwb_ref, ka_ref,
                yf_ref, yb_ref, sf_ref, sb_ref, *, n_batch, n_heads, n_chunks):
    @pl.when(pl.program_id(2) == 0)
    def _():
        sf_ref[...] = jnp.zeros_like(sf_ref)
        sb_ref[...] = jnp.zeros_like(sb_ref)

    C = WKV_CHUNK

    def body(j, carry):
        jb = n_chunks - 1 - j
        rows_f = pl.ds(pl.multiple_of(j * C, C), C)
        rows_b = pl.ds(pl.multiple_of(jb * C, C), C)
        dirs = []
        for bi in range(n_batch):
            dirs.append(dict(bi=bi, rows=rows_f, y_ref=yf_ref, s_ref=sf_ref,
                             ops=_wkv_operands(bi, rows_f, j, False, rf_ref, kf_ref, vf_ref, vtf_ref, kkf_ref,
                                               af_ref, lwf_ref, ka_ref)))
            dirs.append(dict(bi=bi, rows=rows_b, y_ref=yb_ref, s_ref=sb_ref,
                             ops=_wkv_operands(bi, rows_b, jb, True, rb_ref, kb_ref, vb_ref, vtb_ref, kkb_ref,
                                               ab_ref, lwb_ref, ka_ref)))
        _wkv_step(dirs, n_heads // 2)
        return carry

    lax.fori_loop(0, n_chunks, body, 0)


def _wkv(r, k, v, vt, kk, a_f, a_b, lw_f, lw_b, k_a):
    B, T, _ = r.shape
    tt = min(TILE_WKV, T)
    hg = WKV_HEADS_PER_STEP
    nb = WKV_BATCH_PER_STEP if B % WKV_BATCH_PER_STEP == 0 else 1
    nt = T // tt
    ng = RWKV_HEADS // hg
    w = hg * HEAD_DIM
    cpt = tt // WKV_CHUNK
    fwd = pl.BlockSpec((nb, tt, w), lambda b, g, i: (b, i, g))
    bwd = pl.BlockSpec((nb, tt, w), lambda b, g, i: (b, nt - 1 - i, g))
    vt_fwd = pl.BlockSpec((nb, cpt, w, 2 * WKV_CHUNK), lambda b, g, i: (b, i, g, 0))
    vt_bwd = pl.BlockSpec((nb, cpt, w, 2 * WKV_CHUNK), lambda b, g, i: (b, nt - 1 - i, g, 0))
    ka_spec = pl.BlockSpec((1, w), lambda b, g, i: (0, g))
    kern = functools.partial(_wkv_kernel, n_batch=nb, n_heads=hg, n_chunks=cpt)
    return pl.pallas_call(
        kern,
        grid=(B // nb, ng, nt),
        in_specs=[fwd] * 3 + [vt_fwd] + [fwd] * 3 + [bwd] * 3 + [vt_bwd] + [bwd] * 3 + [ka_spec],
        out_specs=[fwd, bwd],
        out_shape=[jax.ShapeDtypeStruct((B, T, D_RWKV), BF16)] * 2,
        scratch_shapes=[pltpu.VMEM((nb, hg // 2, HEAD_DIM, 2 * HEAD_DIM), F32)] * 2,
        compiler_params=pltpu.CompilerParams(
            dimension_semantics=("parallel", "parallel", "arbitrary"),
            vmem_limit_bytes=VMEM_LIMIT),
        name="wkv",
    )(r, k, v, vt, kk, a_f, lw_f, r, k, v, vt, kk, a_b, lw_b, k_a)


def _mla_prep_kernel(z_ref, qg_ref, kvg_ref, wq_ref, wk_ref, wvt_ref, e2_ref,
                     cosq_ref, sinq_ref, cs_ref, q_out, k_out, vt_out):
    z = z_ref[0].astype(F32)
    c_q = z[:, :Q_LORA_RANK]
    c_kv = z[:, Q_LORA_RANK:Q_LORA_RANK + KV_LORA_RANK]
    kr = z[:, Q_LORA_RANK + KV_LORA_RANK:]
    nq = _rms(c_q, qg_ref[...]).astype(BF16)
    nkv = _rms(c_kv, kvg_ref[...]).astype(BF16)
    q = jnp.dot(nq, wq_ref[...], preferred_element_type=F32)
    kf = (jnp.dot(nkv, wk_ref[...], preferred_element_type=F32)
          + _dot(kr * cs_ref[...], e2_ref[...]))
    vt = lax.dot_general(wvt_ref[...], nkv, (((1,), (1,)), ((), ())), preferred_element_type=F32)
    cosq = cosq_ref[...]
    sinq = sinq_ref[...]
    vt_out[0] = vt.astype(vt_out.dtype)
    for h in range(MLA_HEADS):
        qh = q[:, h * QK_PAD:(h + 1) * QK_PAD]
        q_rot = pltpu.roll(qh, QK_PAD - MLA_ROPE_DIM, 1)
        q_out[0, h] = (qh * cosq + q_rot * sinq).astype(q_out.dtype)
        k_out[0, h] = kf[:, h * QK_PAD:(h + 1) * QK_PAD].astype(k_out.dtype)


def _mla_prep(z_m, params, tabs):
    B, T, _ = z_m.shape
    tt = min(TILE_MLA_PREP, T)
    full = lambda a: pl.BlockSpec(a.shape, lambda b, i: (0,) * a.ndim)
    tab = lambda a: pl.BlockSpec((tt, a.shape[1]), lambda b, i: (i, 0))
    hspec = pl.BlockSpec((1, MLA_HEADS, tt, QK_PAD), lambda b, i: (b, 0, i, 0))
    vt_rows = D_MLA
    return pl.pallas_call(
        _mla_prep_kernel,
        grid=(B, T // tt),
        in_specs=[pl.BlockSpec((1, tt, MLA_COLS_PAD), lambda b, i: (b, i, 0))]
        + [full(p) for p in params] + [tab(t) for t in tabs],
        out_specs=[hspec, hspec, pl.BlockSpec((1, vt_rows, tt), lambda b, i: (b, 0, i))],
        out_shape=[jax.ShapeDtypeStruct((B, MLA_HEADS, T, QK_PAD), BF16),
                   jax.ShapeDtypeStruct((B, MLA_HEADS, T, QK_PAD), BF16),
                   jax.ShapeDtypeStruct((B, vt_rows, T), BF16)],
        compiler_params=pltpu.CompilerParams(dimension_semantics=("parallel", "parallel"),
                                             vmem_limit_bytes=VMEM_LIMIT),
        name="mla_prep",
    )(z_m, *params, *tabs)


def _attn_kernel(q_ref, k_ref, vt_ref, og_ref, o_ref):
    def scores(h):
        return _dot_nt(k_ref[0, h], q_ref[0, h])

    outs = []
    s_next = scores(0)
    for h in range(MLA_HEADS):
        s = s_next
        if h + 1 < MLA_HEADS:
            s_next = scores(h + 1)
        m = jnp.max(s, axis=0, keepdims=True)
        p = jnp.exp2(s - m)
        l = jnp.sum(p, axis=0, keepdims=True)
        o = jnp.dot(vt_ref[0, h * MLA_V_DIM:(h + 1) * MLA_V_DIM, :], p.astype(BF16),
                    preferred_element_type=F32)
        outs.append(o / l)
    o = jnp.concatenate(outs, axis=0)
    y = o * lax.rsqrt(jnp.mean(o * o, axis=0, keepdims=True) + NORM_EPS) * og_ref[...]
    o_ref[0] = y.T.astype(o_ref.dtype)


def _attn(q, k, vt, out_g_col):
    B, H, T, _ = q.shape
    tq = min(TILE_ATTN_Q, T)
    return pl.pallas_call(
        _attn_kernel,
        grid=(B, T // tq),
        in_specs=[pl.BlockSpec((1, H, tq, QK_PAD), lambda b, i: (b, 0, i, 0)),
                  pl.BlockSpec((1, H, T, QK_PAD), lambda b, i: (b, 0, 0, 0)),
                  pl.BlockSpec((1, vt.shape[1], T), lambda b, i: (b, 0, 0)),
                  pl.BlockSpec(out_g_col.shape, lambda b, i: (0, 0))],
        out_specs=pl.BlockSpec((1, tq, D_MLA), lambda b, i: (b, i, 0)),
        out_shape=jax.ShapeDtypeStruct((B, T, D_MLA), BF16),
        compiler_params=pltpu.CompilerParams(dimension_semantics=("parallel", "arbitrary"),
                                             vmem_limit_bytes=VMEM_LIMIT),
        name="attn",
    )(q, k, vt, out_g_col)


def _mix_out_kernel(yf_ref, yb_ref, bonus_ref, g_ref, ym_ref, x_ref, lng_ref, lnb_ref, ones_ref,
                    wo_r_ref, wo_m_ref, fg_ref, h_out, n_out):
    y = yf_ref[...].astype(F32) + yb_ref[...].astype(F32)
    ones_bd = ones_ref[...]
    inv_n = 1.0 / HEAD_DIM
    mu = _dot(y, ones_bd) * inv_n
    yc = y - mu
    var = _dot(yc * yc, ones_bd) * inv_n
    gn = yc * lax.rsqrt(var + GN_EPS) * lng_ref[...] + lnb_ref[...]
    y_rwkv = (gn + bonus_ref[...].astype(F32)) * g_ref[...].astype(F32)
    h = (x_ref[...] + _dot(y_rwkv, wo_r_ref[...])
         + jnp.dot(ym_ref[...], wo_m_ref[...], preferred_element_type=F32))
    h_out[...] = h
    n_out[...] = _rms(h, fg_ref[...]).astype(n_out.dtype)


def _mix_out(y_f, y_b, bonus, g, y_mla, x2, params):
    n_tok = x2.shape[0]
    tm = min(TILE_MIX_OUT, n_tok)
    full = lambda a: pl.BlockSpec(a.shape, lambda i: (0,) * a.ndim)
    tok = lambda w: pl.BlockSpec((tm, w), lambda i: (i, 0))
    return pl.pallas_call(
        _mix_out_kernel,
        grid=(n_tok // tm,),
        in_specs=[tok(D_RWKV)] * 4 + [tok(D_MLA), tok(D_MODEL)] + [full(p) for p in params],
        out_specs=[tok(D_MODEL), tok(D_MODEL)],
        out_shape=[jax.ShapeDtypeStruct((n_tok, D_MODEL), F32),
                   jax.ShapeDtypeStruct((n_tok, D_MODEL), BF16)],
        compiler_params=pltpu.CompilerParams(dimension_semantics=("parallel",),
                                             vmem_limit_bytes=VMEM_LIMIT),
        name="mix_out",
    )(y_f, y_b, bonus, g, y_mla, x2, *params)


def _ffn_kernel(n_ref, h_ref, wg_ref, wv_ref, cg_ref, cv_ref, bg_ref, bv_ref, wd_ref, fg_ref,
                o_ref, npad_ref, act_ref):
    i = pl.program_id(1)
    T = n_ref.shape[1]
    R = o_ref.shape[1]
    H = FFN_HALO

    @pl.when(i == 0)
    def _():
        zeros = jnp.zeros((H, D_MODEL), npad_ref.dtype)
        npad_ref[0:H, :] = zeros
        npad_ref[H + T:H + T + H, :] = zeros
        npad_ref[H:H + T, :] = n_ref[0]

    n_win = npad_ref[pl.ds(pl.multiple_of(i * R, R), R + 2 * H), :]

    def conv(u, c, b):
        return c[0:1] * u[H - 1:H - 1 + R] + c[1:2] * u[H:H + R] + c[2:3] * u[H + 1:H + 1 + R] + b

    fc = FFN_COL_CHUNK
    for j in range(D_FF // fc):
        cs = slice(j * fc, (j + 1) * fc)
        gate = conv(jnp.dot(n_win, wg_ref[:, cs], preferred_element_type=F32), cg_ref[:, cs], bg_ref[:, cs])
        val = conv(jnp.dot(n_win, wv_ref[:, cs], preferred_element_type=F32), cv_ref[:, cs], bv_ref[:, cs])
        act_ref[:, cs] = (gate * _sigmoid(gate) * val).astype(act_ref.dtype)
    f = jnp.dot(act_ref[...], wd_ref[...], preferred_element_type=F32)
    o_ref[0] = _rms(h_ref[0] + f, fg_ref[...])


def _ffn(n2, h1, w_gate, w_val, c_gate, c_val, b_gate, b_val, w_down, final_g):
    B, T, _ = n2.shape
    R = min(TILE_FFN, T)
    once = lambda a: pl.BlockSpec(a.shape, lambda b, i: (0,) * a.ndim, pipeline_mode=pl.Buffered(1))
    rows = pl.BlockSpec((1, R, D_MODEL), lambda b, i: (b, i, 0))
    return pl.pallas_call(
        _ffn_kernel,
        grid=(B, T // R),
        in_specs=[pl.BlockSpec((1, T, D_MODEL), lambda b, i: (b, 0, 0)), rows]
        + [once(a) for a in (w_gate, w_val, c_gate, c_val, b_gate, b_val, w_down, final_g)],
        out_specs=rows,
        out_shape=jax.ShapeDtypeStruct((B, T, D_MODEL), F32),
        scratch_shapes=[pltpu.VMEM((T + 2 * FFN_HALO, D_MODEL), BF16),
                        pltpu.VMEM((R, D_FF), BF16)],
        compiler_params=pltpu.CompilerParams(dimension_semantics=("parallel", "arbitrary"),
                                             vmem_limit_bytes=VMEM_LIMIT),
        name="ffn",
    )(n2, h1, w_gate, w_val, c_gate, c_val, b_gate, b_val, w_down, final_g)


def _rot_cols(w):
    half = w.shape[-1] // 2
    return jnp.concatenate([-w[..., half:], w[..., :half]], axis=-1)


def _block_diag2(a, b):
    za = jnp.zeros((a.shape[0], b.shape[1]), a.dtype)
    zb = jnp.zeros((b.shape[0], a.shape[1]), b.dtype)
    return jnp.concatenate([jnp.concatenate([a, za], axis=1), jnp.concatenate([zb, b], axis=1)], axis=0)


def _head_ones():
    idx = np.arange(D_RWKV) // HEAD_DIM
    return jnp.asarray((idx[:, None] == idx[None, :]).astype(np.float32), dtype=BF16)


def _rope_tables(T):
    inv_freq = jnp.power(ROPE_THETA, -jnp.arange(0, MLA_ROPE_DIM, 2, dtype=F32) / MLA_ROPE_DIM)
    ang = jnp.arange(T, dtype=F32)[:, None] * inv_freq[None, :]
    ang = jnp.concatenate([ang, ang], axis=-1)
    cos, sin = jnp.cos(ang), jnp.sin(ang)
    one = jnp.ones((T, MLA_NOPE_DIM), F32)
    zn = jnp.zeros((T, MLA_NOPE_DIM), F32)
    zp = jnp.zeros((T, QK_PAD - MLA_NOPE_DIM - MLA_ROPE_DIM), F32)
    q_scale = MLA_SCALE * math.log2(math.e)
    cosq = jnp.concatenate([one, cos, zp], axis=1) * q_scale
    sinq = jnp.concatenate([zn, sin, zp], axis=1) * q_scale
    cs = jnp.concatenate([cos, sin, jnp.zeros((T, 128 - 2 * MLA_ROPE_DIM), F32)], axis=1)
    return cosq, sinq, cs


def _pad_heads(w, src_width, src_off, n, dst_off):
    K = w.shape[0]
    w = w.reshape(K, MLA_HEADS, src_width)[:, :, src_off:src_off + n]
    out = jnp.zeros((K, MLA_HEADS, QK_PAD), w.dtype)
    out = out.at[:, :, dst_off:dst_off + n].set(w)
    return out.reshape(K, MLA_HEADS * QK_PAD)


def kernel(x, ln_mix_g, w_in, shift_mu_prev, shift_mu_next, decay_w0_fwd, decay_w2_fwd, decay_w0_bwd, decay_w2_bwd, iclr_a0_fwd, iclr_a2_fwd, iclr_a0_bwd, iclr_a2_bwd, gate_g2, k_k, k_a, r_k, ln_x_g, ln_x_b, q_norm_g, w_uq, kv_norm_g, w_ukv, mla_out_g, w_out, ln_ffn_g, w_ffn_up, ffn_conv_w, ffn_conv_b, w_ffn_down, ln_final_g):
    B, T, D = x.shape
    assert w_in.shape[0] == 1, "single-layer block"
    n_tok = B * T
    row = lambda a: a.reshape(1, -1).astype(F32)
    l = 0

    w_in_l = w_in[l]
    w_r = w_in_l[:, :RWKV_COLS].astype(BF16)
    w_kr = w_in_l[:, RWKV_COLS + Q_LORA_RANK + KV_LORA_RANK:]
    w_m = jnp.concatenate([w_in_l[:, RWKV_COLS:], _rot_cols(w_kr),
                           jnp.zeros((D, MLA_COLS_PAD - (Q_LORA_RANK + KV_LORA_RANK + 2 * MLA_ROPE_DIM)), F32)],
                          axis=1).astype(BF16)
    ones_bd = _head_ones()
    prep_params = [
        row(shift_mu_prev[l]), row(shift_mu_next[l]),
        row(jnp.concatenate([decay_w0_fwd[l], decay_w0_bwd[l]])),
        _block_diag2(decay_w2_fwd[l], decay_w2_bwd[l]).astype(BF16),
        row(jnp.concatenate([iclr_a0_fwd[l], iclr_a0_bwd[l]])),
        _block_diag2(iclr_a2_fwd[l], iclr_a2_bwd[l]).astype(BF16),
        gate_g2[l].astype(BF16),
        row(k_k[l]), row(k_a[l]), row(r_k[l]), ones_bd, jnp.eye(D_RWKV, dtype=BF16),
    ]
    qw = MLA_NOPE_DIM + MLA_ROPE_DIM
    w_uq_l = w_uq[l]
    w_uq_h = w_uq_l.reshape(Q_LORA_RANK, MLA_HEADS, qw)
    wq_all = jnp.concatenate([w_uq_h, _rot_cols(w_uq_h[:, :, MLA_NOPE_DIM:])], axis=-1)
    wq_all = wq_all.reshape(Q_LORA_RANK, MLA_HEADS * QK_PAD).astype(BF16)
    kvw = MLA_NOPE_DIM + MLA_V_DIM
    w_ukv_l = w_ukv[l]
    wk_pad = _pad_heads(w_ukv_l, kvw, 0, MLA_NOPE_DIM, 0).astype(BF16)
    wv_t = w_ukv_l.reshape(KV_LORA_RANK, MLA_HEADS, kvw)[:, :, MLA_NOPE_DIM:].reshape(KV_LORA_RANK, D_MLA).T.astype(BF16)
    e2 = np.zeros((128, MLA_HEADS * QK_PAD), np.float32)
    for h in range(MLA_HEADS):
        for j in range(MLA_ROPE_DIM):
            e2[j, h * QK_PAD + MLA_NOPE_DIM + j] = 1.0
            e2[MLA_ROPE_DIM + j, h * QK_PAD + MLA_NOPE_DIM + j] = 1.0
    e2 = jnp.asarray(e2, dtype=BF16)
    cosq, sinq, cs = _rope_tables(T)
    mla_params = [row(q_norm_g[l]), row(kv_norm_g[l]), wq_all, wk_pad, wv_t, e2]
    w_out_l = w_out[l]
    mix_params = [row(ln_x_g[l]), row(ln_x_b[l]), ones_bd,
                  w_out_l[:D_RWKV].astype(BF16), w_out_l[D_RWKV:].astype(BF16), row(ln_ffn_g[l])]
    w_up = w_ffn_up[l]
    conv_w = ffn_conv_w[l]
    conv_b = ffn_conv_b[l]

    x2 = x.reshape(n_tok, D)
    z_r, z_m = _inproj(x2, row(ln_mix_g[l]), w_r, w_m)
    z_r = z_r.reshape(B, T, RWKV_COLS)
    z_m = z_m.reshape(B, T, MLA_COLS_PAD)

    r, k, v, vt, kk, a_f, a_b, lw_f, lw_b, bonus, g = _rwkv_prep(z_r, prep_params)
    y_f, y_b = _wkv(r, k, v, vt, kk, a_f, a_b, lw_f, lw_b, row(k_a[l]))

    q, kf, vt_mla = _mla_prep(z_m, mla_params, [cosq, sinq, cs])
    y_mla = _attn(q, kf, vt_mla, mla_out_g[l].reshape(D_MLA, 1).astype(F32))

    flat = lambda a: a.reshape(n_tok, a.shape[-1])
    h1, n2 = _mix_out(flat(y_f), flat(y_b), flat(bonus), flat(g), flat(y_mla), x2, mix_params)

    return _ffn(n2.reshape(B, T, D), h1.reshape(B, T, D),
                w_up[:, :D_FF].astype(BF16), w_up[:, D_FF:].astype(BF16),
                conv_w[:, :D_FF], conv_w[:, D_FF:], row(conv_b[:D_FF]), row(conv_b[D_FF:]),
                w_ffn_down[l].astype(BF16), row(ln_final_g))
```

```python
import functools
import math

import numpy as np
import jax
import jax.numpy as jnp
from jax import lax
from jax.experimental import pallas as pl
from jax.experimental.pallas import tpu as pltpu

F32 = jnp.float32
BF16 = jnp.bfloat16

D_MODEL = 1024
RWKV_HEADS = 8
HEAD_DIM = 64
D_RWKV = RWKV_HEADS * HEAD_DIM
DECAY_LORA = 64
ICLR_LORA = 64
GATE_LORA = 128
GN_EPS = 64e-5
L2_EPS = 1e-12
MLA_HEADS = 8
MLA_NOPE_DIM = 64
MLA_ROPE_DIM = 32
MLA_V_DIM = 64
D_MLA = MLA_HEADS * MLA_V_DIM
Q_LORA_RANK = 768
KV_LORA_RANK = 256
ROPE_THETA = 10000.0
MLA_SCALE = (MLA_NOPE_DIM + MLA_ROPE_DIM) ** -0.5
RWKV_COLS = 3 * D_RWKV + 2 * DECAY_LORA + 2 * ICLR_LORA + GATE_LORA
MLA_COLS_PAD = 1152
D_FF = 2816
NORM_EPS = 1e-6
QK_PAD = 128
VMEM_LIMIT = 56 * 1024 * 1024
BF16_SUBLANES = 16

WKV_CHUNK = 64
DECAY_LOG_SCALE = -math.exp(-0.5)

TILE_INPROJ = 1024
TILE_RWKV_PREP = 256
TILE_WKV = 512
WKV_HEADS_PER_STEP = 8
WKV_BATCH_PER_STEP = 2
TILE_MLA_PREP = 1024
TILE_ATTN_Q = 512
TILE_MIX_OUT = 1024
TILE_FFN = 512
FFN_COL_CHUNK = 256
FFN_HALO = BF16_SUBLANES


def _dot(a, b):
    return jnp.dot(a.astype(BF16), b.astype(BF16), preferred_element_type=F32)


def _dot_nt(a, b):
    return lax.dot_general(a.astype(BF16), b.astype(BF16), (((1,), (1,)), ((), ())),
                           preferred_element_type=F32)


def _sigmoid(x):
    return 1.0 / (1.0 + jnp.exp(-x))


def _rms(x, g):
    return x * lax.rsqrt(jnp.mean(x * x, axis=-1, keepdims=True) + NORM_EPS) * g


def _inproj_kernel(x_ref, g_ref, wr_ref, wm_ref, zr_ref, zm_ref):
    n = _rms(x_ref[...], g_ref[...]).astype(BF16)
    zr_ref[...] = jnp.dot(n, wr_ref[...], preferred_element_type=F32).astype(zr_ref.dtype)
    zm_ref[...] = jnp.dot(n, wm_ref[...], preferred_element_type=F32).astype(zm_ref.dtype)


def _inproj(x2, g, w_r, w_m):
    n_tok = x2.shape[0]
    tm = min(TILE_INPROJ, n_tok)
    full = lambda a: pl.BlockSpec(a.shape, lambda i: (0,) * a.ndim)
    return pl.pallas_call(
        _inproj_kernel,
        grid=(n_tok // tm,),
        in_specs=[pl.BlockSpec((tm, D_MODEL), lambda i: (i, 0)), full(g), full(w_r), full(w_m)],
        out_specs=[pl.BlockSpec((tm, RWKV_COLS), lambda i: (i, 0)),
                   pl.BlockSpec((tm, MLA_COLS_PAD), lambda i: (i, 0))],
        out_shape=[jax.ShapeDtypeStruct((n_tok, RWKV_COLS), BF16),
                   jax.ShapeDtypeStruct((n_tok, MLA_COLS_PAD), BF16)],
        compiler_params=pltpu.CompilerParams(dimension_semantics=("parallel",),
                                             vmem_limit_bytes=VMEM_LIMIT),
        name="inproj",
    )(x2, g, w_r, w_m)


def _rwkv_prep_kernel(z_ref, zp_ref, zn_ref, mup_ref, mun_ref, w0_ref, w2_ref, a0_ref, a2_ref,
                      g2_ref, kk_ref, ka_ref, rk_ref, ones_ref, dprev_ref, dnext_ref,
                      r_out, k_out, v_out, kkn_out, af_out, ab_out, lwf_out, lwb_out,
                      bonus_out, g_out):
    i = pl.program_id(1)
    nt = pl.num_programs(1)
    z16 = z_ref[0]
    z = z16.astype(F32)
    tt = z.shape[0]
    has_prev = (i > 0).astype(F32)
    has_next = (i < nt - 1).astype(F32)
    prev_row = zp_ref[0][BF16_SUBLANES - 1:BF16_SUBLANES, :].astype(F32) * has_prev
    next_row = zn_ref[0][0:1, :].astype(F32) * has_next
    d_prev = jnp.dot(dprev_ref[...], z16, preferred_element_type=F32)
    d_next = jnp.dot(dnext_ref[...], z16, preferred_element_type=F32)
    row8 = lax.broadcasted_iota(jnp.int32, (8, 1), 0)
    d_prev = jnp.concatenate([d_prev[:8] + jnp.where(row8 == 0, prev_row, 0.0), d_prev[8:]], axis=0)
    d_next = jnp.concatenate([d_next[:tt - 8], d_next[tt - 8:] + jnp.where(row8 == 7, next_row, 0.0)], axis=0)
    zs = z + mup_ref[...] * d_prev + mun_ref[...] * d_next

    d = D_RWKV
    r = zs[:, 0:d]
    k = zs[:, d:2 * d]
    v = zs[:, 2 * d:3 * d]
    wd = zs[:, 3 * d:3 * d + 128]
    ad = zs[:, 3 * d + 128:3 * d + 256]
    gd = zs[:, 3 * d + 256:3 * d + 384]

    logit = w0_ref[...] + _dot(jnp.tanh(wd), w2_ref[...])
    logw = DECAY_LOG_SCALE * _sigmoid(logit)
    a = _sigmoid(a0_ref[...] + _dot(ad, a2_ref[...]))
    g = _dot(_sigmoid(gd), g2_ref[...])
    a_f = a[:, :d]
    a_b = a[:, d:]

    ones_bd = ones_ref[...]
    kk = k * kk_ref[...]
    ss = _dot(kk * kk, ones_bd)
    kk = kk * lax.rsqrt(jnp.maximum(ss, L2_EPS * L2_EPS))
    k_a = ka_ref[...]
    k_fb = k * (2.0 + (a_f + a_b - 2.0) * k_a)
    bonus = _dot(r * k_fb * rk_ref[...], ones_bd) * v

    r_out[0] = r.astype(r_out.dtype)
    k_out[0] = k.astype(k_out.dtype)
    v_out[0] = v.astype(v_out.dtype)
    kkn_out[0] = kk.astype(kkn_out.dtype)
    af_out[0] = a_f.astype(af_out.dtype)
    ab_out[0] = a_b.astype(ab_out.dtype)
    lwf_out[0] = logw[:, :d]
    lwb_out[0] = logw[:, d:]
    bonus_out[0] = bonus.astype(bonus_out.dtype)
    g_out[0] = g.astype(g_out.dtype)


def _rwkv_prep(z_r, params):
    B, T, _ = z_r.shape
    tt = min(TILE_RWKV_PREP, T)
    nt = T // tt
    hr = BF16_SUBLANES
    hb = tt // hr
    full = lambda a: pl.BlockSpec(a.shape, lambda b, i: (0,) * a.ndim)
    seq = lambda w: pl.BlockSpec((1, tt, w), lambda b, i: (b, i, 0))
    in_specs = [
        seq(RWKV_COLS),
        pl.BlockSpec((1, hr, RWKV_COLS), lambda b, i: (b, jnp.maximum(i * hb - 1, 0), 0)),
        pl.BlockSpec((1, hr, RWKV_COLS), lambda b, i: (b, jnp.minimum((i + 1) * hb, T // hr - 1), 0)),
    ] + [full(p) for p in params]
    seq16 = jax.ShapeDtypeStruct((B, T, D_RWKV), BF16)
    seq32 = jax.ShapeDtypeStruct((B, T, D_RWKV), F32)
    return pl.pallas_call(
        _rwkv_prep_kernel,
        grid=(B, nt),
        in_specs=in_specs,
        out_specs=[seq(D_RWKV)] * 10,
        out_shape=[seq16, seq16, seq16, seq16, seq16, seq16, seq32, seq32, seq16, seq16],
        compiler_params=pltpu.CompilerParams(dimension_semantics=("parallel", "parallel"),
                                             vmem_limit_bytes=VMEM_LIMIT),
        name="rwkv_prep",
    )(z_r, z_r, z_r, *params)


def _wkv_operands(bi, rows, reverse, r_ref, k_ref, v_ref, kk_ref, a_ref, lw_ref, ka_ref):
    C = WKV_CHUNK
    lw = lw_ref[bi, rows, :]
    ti = lax.broadcasted_iota(jnp.int32, (C, C), 0)
    si = lax.broadcasted_iota(jnp.int32, (C, C), 1)
    if reverse:
        incl = si >= ti
        strict = si > ti
    else:
        incl = si <= ti
        strict = si < ti
    tri = incl.astype(BF16)
    lw_hi = lw.astype(BF16)
    lw_lo = (lw - lw_hi.astype(F32)).astype(BF16)
    cum = (jnp.dot(tri, lw_hi, preferred_element_type=F32)
           + jnp.dot(tri, lw_lo, preferred_element_type=F32))
    e_inc = jnp.exp(cum)
    e_dec = jnp.exp(-cum)
    e_exc = jnp.exp(cum - lw)
    r = r_ref[bi, rows, :].astype(F32)
    k = k_ref[bi, rows, :].astype(F32)
    kk = kk_ref[bi, rows, :].astype(F32)
    a = a_ref[bi, rows, :].astype(F32)
    k_dir = k * (1.0 + (a - 1.0) * ka_ref[...])
    t2 = lax.broadcasted_iota(jnp.int32, (C, 2 * C), 0)
    l2 = lax.broadcasted_iota(jnp.int32, (C, 2 * C), 1)
    s2 = jnp.bitwise_and(l2, C - 1)
    hi = l2 >= C
    if reverse:
        incl2 = s2 >= t2
        strict2 = s2 > t2
    else:
        incl2 = s2 <= t2
        strict2 = s2 < t2
    return dict(
        incl2=incl2, strict2=strict2, lo=jnp.logical_not(hi), hi=hi,
        sign_lo_neg=jnp.where(hi, 1.0, -1.0).astype(F32), eye2=(s2 == t2).astype(F32),
        p_end=jnp.exp(jnp.sum(lw, axis=0, keepdims=True)),
        alpha=(e_exc * kk).astype(BF16), beta=(e_dec * kk * a).astype(BF16),
        kappa=(e_dec * k_dir).astype(BF16), rho=(e_inc * r).astype(BF16),
        v=v_ref[bi, rows, :])


def _wkv_step(dirs, n_pairs):
    C = WKV_CHUNK
    W = 2 * C
    cat = jnp.concatenate
    zc = jnp.zeros((C, W), BF16)
    lo2 = lax.broadcasted_iota(jnp.int32, (W, W), 1) < C

    def keep(mask, a):
        return jnp.where(mask, a, jnp.zeros_like(a))

    def bd(a, lo, hi):
        return cat([keep(lo, a), keep(hi, a)], axis=0)

    def bd2(a_e, a_o):
        return cat([cat([a_e, zc], axis=1), cat([zc, a_o], axis=1)], axis=0)

    chains = []
    for d in dirs:
        ops = d["ops"]
        for pi in range(n_pairs):
            sl = slice(pi * W, (pi + 1) * W)
            v = ops["v"][:, sl]
            vt = cat([keep(ops["hi"], v), keep(ops["lo"], v)], axis=0).astype(F32).T.astype(BF16)
            chains.append(dict(d=d, pi=pi, sl=sl, al=ops["alpha"][:, sl], be=ops["beta"][:, sl],
                               ka=ops["kappa"][:, sl], rh=ops["rho"][:, sl], v=v,
                               vt_e=vt[:C], vt_o=vt[C:]))
    for c in chains:
        ar = cat([c["al"], c["rh"]], axis=0)
        c["g_e"] = _dot_nt(keep(lo2, ar), cat([c["be"], c["ka"]], axis=0))
        c["g_o"] = _dot_nt(keep(jnp.logical_not(lo2), ar), cat([c["ka"], c["be"]], axis=0))
    for c in chains:
        ops = c["d"]["ops"]
        lo, hi, strict2, incl2 = ops["lo"], ops["hi"], ops["strict2"], ops["incl2"]
        ga_e, gr_e = c["g_e"][:C], c["g_e"][C:]
        ga_o, gr_o = c["g_o"][:C], c["g_o"][C:]
        c["m"] = jnp.where(strict2, jnp.where(lo, -ga_e, -ga_o), 0.0).astype(BF16)
        ak_e = jnp.where(strict2, ga_e, 0.0).astype(BF16)
        ak_o = jnp.where(strict2, ga_o, 0.0).astype(BF16)
        c["wt_rhs"] = bd2(jnp.where(lo, c["al"], ak_e), jnp.where(hi, c["al"], ak_o))
        rbk_e = jnp.where(incl2, gr_e, 0.0) * ops["sign_lo_neg"]
        rbk_o = jnp.where(incl2, gr_o, 0.0) * (-ops["sign_lo_neg"])
        c["rbk"] = cat([rbk_e, rbk_o], axis=1).astype(BF16)
        c["P"] = ops["eye2"] + c["m"].astype(F32)
    for c in chains:
        ops = c["d"]["ops"]
        c["mp"] = jnp.dot(c["m"], bd(c["m"], ops["lo"], ops["hi"]), preferred_element_type=F32)
    n_sq = int(math.log2(C)) - 1
    for s in range(n_sq):
        last = s == n_sq - 1
        for c in chains:
            ops = c["d"]["ops"]
            mp = c["mp"].astype(BF16)
            p16 = c["P"].astype(BF16)
            out = jnp.dot(p16 if last else cat([p16, mp], axis=0), bd(mp, ops["lo"], ops["hi"]),
                          preferred_element_type=F32)
            c["P"] = c["P"] + out[:C]
            if not last:
                c["mp"] = out[C:]
    for c in chains:
        wt = _dot(c["P"], c["wt_rhs"]).astype(BF16)
        c["wt_e"], c["wt_o"] = wt[:, :W], wt[:, W:]
    for c in chains:
        ops = c["d"]["ops"]
        lo, hi = ops["lo"], ops["hi"]
        c["S"] = c["d"]["s_ref"][c["d"]["bi"], c["pi"]]
        s16 = c["S"].astype(BF16)
        sv_e = keep(lo, s16) + c["vt_e"]
        sv_o = keep(hi, s16) + c["vt_o"]
        lhs = cat([cat([c["wt_e"], keep(lo, c["rh"])], axis=0),
                   cat([c["wt_o"], keep(hi, c["rh"])], axis=0)], axis=1)
        c["x"] = _dot_nt(lhs, bd2(sv_e, sv_o))
        c["ut"] = _dot_nt(cat([sv_e, sv_o], axis=1), bd2(c["wt_e"], c["wt_o"]))
    for c in chains:
        ops = c["d"]["ops"]
        lo, hi = ops["lo"], ops["hi"]
        u = c["x"][:C].astype(BF16)
        vu = cat([keep(lo, u), keep(lo, c["v"]), keep(hi, c["v"]), keep(hi, u)], axis=0)
        c["y"] = c["x"][C:] + jnp.dot(c["rbk"], vu, preferred_element_type=F32)
        ut = c["ut"].astype(BF16)
        lhs = cat([c["vt_e"] - keep(lo, ut), c["vt_o"] - keep(hi, ut)], axis=1)
        kb = cat([keep(lo, c["be"]), keep(lo, c["ka"]), keep(hi, c["ka"]), keep(hi, c["be"])], axis=0)
        c["ds"] = jnp.dot(lhs, kb, preferred_element_type=F32)
    for c in chains:
        c["d"]["s_ref"][c["d"]["bi"], c["pi"]] = (c["S"] + c["ds"]) * c["d"]["ops"]["p_end"][:, c["sl"]]
    for d in dirs:
        d["y_ref"][d["bi"], d["rows"], :] =cat([c["y"] for c in chains if c["d"] is d], axis=1).astype(d["y_ref"].dtype)


def _wkv_kernel(rf_ref, kf_ref, vf_ref, kkf_ref, af_ref, lwf_ref,
                rb_ref, kb_ref, vb_ref, kkb_ref, ab_ref, lwb_ref, ka_ref,
                yf_ref, yb_ref, sf_ref, sb_ref, *, n_batch, n_heads, n_chunks):
    @pl.when(pl.program_id(2) == 0)
    def _():
        sf_ref[...] = jnp.zeros_like(sf_ref)
        sb_ref[...] = jnp.zeros_like(sb_ref)

    C = WKV_CHUNK

    def body(j, carry):
        jb = n_chunks - 1 - j
        rows_f = pl.ds(pl.multiple_of(j * C, C), C)
        rows_b = pl.ds(pl.multiple_of(jb * C, C), C)
        dirs = []
        for bi in range(n_batch):
            dirs.append(dict(bi=bi, rows=rows_f, y_ref=yf_ref, s_ref=sf_ref,
                             ops=_wkv_operands(bi, rows_f, False, rf_ref, kf_ref, vf_ref, kkf_ref,
                                               af_ref, lwf_ref, ka_ref)))
            dirs.append(dict(bi=bi, rows=rows_b, y_ref=yb_ref, s_ref=sb_ref,
                             ops=_wkv_operands(bi, rows_b, True, rb_ref, kb_ref, vb_ref, kkb_ref,
                                               ab_ref, lwb_ref, ka_ref)))
        _wkv_step(dirs, n_heads // 2)
        return carry

    lax.fori_loop(0, n_chunks, body, 0)


def _wkv(r, k, v, kk, a_f, a_b, lw_f, lw_b, k_a):
    B, T, _ = r.shape
    tt = min(TILE_WKV, T)
    hg = WKV_HEADS_PER_STEP
    nb = WKV_BATCH_PER_STEP if B % WKV_BATCH_PER_STEP == 0 else 1
    nt = T // tt
    ng = RWKV_HEADS // hg
    w = hg * HEAD_DIM
    cpt = tt // WKV_CHUNK
    fwd = pl.BlockSpec((nb, tt, w), lambda b, g, i: (b, i, g))
    bwd = pl.BlockSpec((nb, tt, w), lambda b, g, i: (b, nt - 1 - i, g))
    ka_spec = pl.BlockSpec((1, w), lambda b, g, i: (0, g))
    kern = functools.partial(_wkv_kernel, n_batch=nb, n_heads=hg, n_chunks=cpt)
    return pl.pallas_call(
        kern,
        grid=(B // nb, ng, nt),
        in_specs=[fwd] * 6 + [bwd] * 6 + [ka_spec],
        out_specs=[fwd, bwd],
        out_shape=[jax.ShapeDtypeStruct((B, T, D_RWKV), BF16)] * 2,
        scratch_shapes=[pltpu.VMEM((nb, hg // 2, HEAD_DIM, 2 * HEAD_DIM), F32)] * 2,
        compiler_params=pltpu.CompilerParams(
            dimension_semantics=("parallel", "parallel", "arbitrary"),
            vmem_limit_bytes=VMEM_LIMIT),
        name="wkv",
    )(r, k, v, kk, a_f, lw_f, r, k, v, kk, a_b, lw_b, k_a)


def _mla_prep_kernel(z_ref, qg_ref, kvg_ref, wq_ref, wk_ref, wvt_ref, e2_ref,
                     cosq_ref, sinq_ref, cs_ref, q_out, k_out, vt_out):
    z = z_ref[0].astype(F32)
    c_q = z[:, :Q_LORA_RANK]
    c_kv = z[:, Q_LORA_RANK:Q_LORA_RANK + KV_LORA_RANK]
    kr = z[:, Q_LORA_RANK + KV_LORA_RANK:]
    nq = _rms(c_q, qg_ref[...]).astype(BF16)
    nkv = _rms(c_kv, kvg_ref[...]).astype(BF16)
    q = jnp.dot(nq, wq_ref[...], preferred_element_type=F32)
    kf = (jnp.dot(nkv, wk_ref[...], preferred_element_type=F32)
          + _dot(kr * cs_ref[...], e2_ref[...]))
    vt = lax.dot_general(wvt_ref[...], nkv, (((1,), (1,)), ((), ())), preferred_element_type=F32)
    cosq = cosq_ref[...]
    sinq = sinq_ref[...]
    vt_out[0] = vt.astype(vt_out.dtype)
    for h in range(MLA_HEADS):
        qh = q[:, h * QK_PAD:(h + 1) * QK_PAD]
        q_rot = pltpu.roll(qh, QK_PAD - MLA_ROPE_DIM, 1)
        q_out[0, h] = (qh * cosq + q_rot * sinq).astype(q_out.dtype)
        k_out[0, h] = kf[:, h * QK_PAD:(h + 1) * QK_PAD].astype(k_out.dtype)


def _mla_prep(z_m, params, tabs):
    B, T, _ = z_m.shape
    tt = min(TILE_MLA_PREP, T)
    full = lambda a: pl.BlockSpec(a.shape, lambda b, i: (0,) * a.ndim)
    tab = lambda a: pl.BlockSpec((tt, a.shape[1]), lambda b, i: (i, 0))
    hspec = pl.BlockSpec((1, MLA_HEADS, tt, QK_PAD), lambda b, i: (b, 0, i, 0))
    vt_rows = D_MLA
    return pl.pallas_call(
        _mla_prep_kernel,
        grid=(B, T // tt),
        in_specs=[pl.BlockSpec((1, tt, MLA_COLS_PAD), lambda b, i: (b, i, 0))]
        + [full(p) for p in params] + [tab(t) for t in tabs],
        out_specs=[hspec, hspec, pl.BlockSpec((1, vt_rows, tt), lambda b, i: (b, 0, i))],
        out_shape=[jax.ShapeDtypeStruct((B, MLA_HEADS, T, QK_PAD), BF16),
                   jax.ShapeDtypeStruct((B, MLA_HEADS, T, QK_PAD), BF16),
                   jax.ShapeDtypeStruct((B, vt_rows, T), BF16)],
        compiler_params=pltpu.CompilerParams(dimension_semantics=("parallel", "parallel"),
                                             vmem_limit_bytes=VMEM_LIMIT),
        name="mla_prep",
    )(z_m, *params, *tabs)


def _attn_kernel(q_ref, k_ref, vt_ref, og_ref, o_ref):
    def scores(h):
        return _dot_nt(k_ref[0, h], q_ref[0, h])

    outs = []
    s_next = scores(0)
    for h in range(MLA_HEADS):
        s = s_next
        if h + 1 < MLA_HEADS:
            s_next = scores(h + 1)
        m = jnp.max(s, axis=0, keepdims=True)
        p = jnp.exp2(s - m)
        l = jnp.sum(p, axis=0, keepdims=True)
        o = jnp.dot(vt_ref[0, h * MLA_V_DIM:(h + 1) * MLA_V_DIM, :], p.astype(BF16),
                    preferred_element_type=F32)
        outs.append(o / l)
    o = jnp.concatenate(outs, axis=0)
    y = o * lax.rsqrt(jnp.mean(o * o, axis=0, keepdims=True) + NORM_EPS) * og_ref[...]
    o_ref[0] = y.T.astype(o_ref.dtype)


def _attn(q, k, vt, out_g_col):
    B, H, T, _ = q.shape
    tq = min(TILE_ATTN_Q, T)
    return pl.pallas_call(
        _attn_kernel,
        grid=(B, T // tq),
        in_specs=[pl.BlockSpec((1, H, tq, QK_PAD), lambda b, i: (b, 0, i, 0)),
                  pl.BlockSpec((1, H, T, QK_PAD), lambda b, i: (b, 0, 0, 0)),
                  pl.BlockSpec((1, vt.shape[1], T), lambda b, i: (b, 0, 0)),
                  pl.BlockSpec(out_g_col.shape, lambda b, i: (0, 0))],
        out_specs=pl.BlockSpec((1, tq, D_MLA), lambda b, i: (b, i, 0)),
        out_shape=jax.ShapeDtypeStruct((B, T, D_MLA), BF16),
        compiler_params=pltpu.CompilerParams(dimension_semantics=("parallel", "arbitrary"),
                                             vmem_limit_bytes=VMEM_LIMIT),
        name="attn",
    )(q, k, vt, out_g_col)


def _mix_out_kernel(yf_ref, yb_ref, bonus_ref, g_ref, ym_ref, x_ref, lng_ref, lnb_ref, ones_ref,
                    wo_r_ref, wo_m_ref, fg_ref, h_out, n_out):
    y = yf_ref[...].astype(F32) + yb_ref[...].astype(F32)
    ones_bd = ones_ref[...]
    inv_n = 1.0 / HEAD_DIM
    mu = _dot(y, ones_bd) * inv_n
    yc = y - mu
    var = _dot(yc * yc, ones_bd) * inv_n
    gn = yc * lax.rsqrt(var + GN_EPS) * lng_ref[...] + lnb_ref[...]
    y_rwkv = (gn + bonus_ref[...].astype(F32)) * g_ref[...].astype(F32)
    h = (x_ref[...] + _dot(y_rwkv, wo_r_ref[...])
         + jnp.dot(ym_ref[...], wo_m_ref[...], preferred_element_type=F32))
    h_out[...] = h
    n_out[...] = _rms(h, fg_ref[...]).astype(n_out.dtype)


def _mix_out(y_f, y_b, bonus, g, y_mla, x2, params):
    n_tok = x2.shape[0]
    tm = min(TILE_MIX_OUT, n_tok)
    full = lambda a: pl.BlockSpec(a.shape, lambda i: (0,) * a.ndim)
    tok = lambda w: pl.BlockSpec((tm, w), lambda i: (i, 0))
    return pl.pallas_call(
        _mix_out_kernel,
        grid=(n_tok // tm,),
        in_specs=[tok(D_RWKV)] * 4 + [tok(D_MLA), tok(D_MODEL)] + [full(p) for p in params],
        out_specs=[tok(D_MODEL), tok(D_MODEL)],
        out_shape=[jax.ShapeDtypeStruct((n_tok, D_MODEL), F32),
                   jax.ShapeDtypeStruct((n_tok, D_MODEL), BF16)],
        compiler_params=pltpu.CompilerParams(dimension_semantics=("parallel",),
                                             vmem_limit_bytes=VMEM_LIMIT),
        name="mix_out",
    )(y_f, y_b, bonus, g, y_mla, x2, *params)


def _ffn_kernel(n_ref, h_ref, wg_ref, wv_ref, cg_ref, cv_ref, bg_ref, bv_ref, wd_ref, fg_ref,
                o_ref, npad_ref, act_ref):
    i = pl.program_id(1)
    T = n_ref.shape[1]
    R = o_ref.shape[1]
    H = FFN_HALO

    @pl.when(i == 0)
    def _():
        zeros = jnp.zeros((H, D_MODEL), npad_ref.dtype)
        npad_ref[0:H, :] = zeros
        npad_ref[H + T:H + T + H, :] = zeros
        npad_ref[H:H + T, :] = n_ref[0]

    n_win = npad_ref[pl.ds(pl.multiple_of(i * R, R), R + 2 * H), :]

    def conv(u, c, b):
        return c[0:1] * u[H - 1:H - 1 + R] + c[1:2] * u[H:H + R] + c[2:3] * u[H + 1:H + 1 + R] + b

    fc = FFN_COL_CHUNK
    for j in range(D_FF // fc):
        cs = slice(j * fc, (j + 1) * fc)
        gate = conv(jnp.dot(n_win, wg_ref[:, cs], preferred_element_type=F32), cg_ref[:, cs], bg_ref[:, cs])
        val = conv(jnp.dot(n_win, wv_ref[:, cs], preferred_element_type=F32), cv_ref[:, cs], bv_ref[:, cs])
        act_ref[:, cs] = (gate * _sigmoid(gate) * val).astype(act_ref.dtype)
    f = jnp.dot(act_ref[...], wd_ref[...], preferred_element_type=F32)
    o_ref[0] = _rms(h_ref[0] + f, fg_ref[...])


def _ffn(n2, h1, w_gate, w_val, c_gate, c_val, b_gate, b_val, w_down, final_g):
    B, T, _ = n2.shape
    R = min(TILE_FFN, T)
    once = lambda a: pl.BlockSpec(a.shape, lambda b, i: (0,) * a.ndim, pipeline_mode=pl.Buffered(1))
    rows = pl.BlockSpec((1, R, D_MODEL), lambda b, i: (b, i, 0))
    return pl.pallas_call(
        _ffn_kernel,
        grid=(B, T // R),
        in_specs=[pl.BlockSpec((1, T, D_MODEL), lambda b, i: (b, 0, 0)), rows]
        + [once(a) for a in (w_gate, w_val, c_gate, c_val, b_gate, b_val, w_down, final_g)],
        out_specs=rows,
        out_shape=jax.ShapeDtypeStruct((B, T, D_MODEL), F32),
        scratch_shapes=[pltpu.VMEM((T + 2 * FFN_HALO, D_MODEL), BF16),
                        pltpu.VMEM((R, D_FF), BF16)],
        compiler_params=pltpu.CompilerParams(dimension_semantics=("parallel", "arbitrary"),
                                             vmem_limit_bytes=VMEM_LIMIT),
        name="ffn",
    )(n2, h1, w_gate, w_val, c_gate, c_val, b_gate, b_val, w_down, final_g)


def _rot_cols(w):
    half = w.shape[-1] // 2
    return jnp.concatenate([-w[..., half:], w[..., :half]], axis=-1)


def _block_diag2(a, b):
    za = jnp.zeros((a.shape[0], b.shape[1]), a.dtype)
    zb = jnp.zeros((b.shape[0], a.shape[1]), b.dtype)
    return jnp.concatenate([jnp.concatenate([a, za], axis=1), jnp.concatenate([zb, b], axis=1)], axis=0)


def _shift_diff_matrices(n):
    eye = np.eye(n, dtype=np.float32)
    d_prev = np.eye(n, k=-1, dtype=np.float32) - eye
    d_next = np.eye(n, k=1, dtype=np.float32) - eye
    return jnp.asarray(d_prev, dtype=BF16), jnp.asarray(d_next, dtype=BF16)


def _head_ones():
    idx = np.arange(D_RWKV) // HEAD_DIM
    return jnp.asarray((idx[:, None] == idx[None, :]).astype(np.float32), dtype=BF16)


def _rope_tables(T):
    inv_freq = jnp.power(ROPE_THETA, -jnp.arange(0, MLA_ROPE_DIM, 2, dtype=F32) / MLA_ROPE_DIM)
    ang = jnp.arange(T, dtype=F32)[:, None] * inv_freq[None, :]
    ang = jnp.concatenate([ang, ang], axis=-1)
    cos, sin = jnp.cos(ang), jnp.sin(ang)
    one = jnp.ones((T, MLA_NOPE_DIM), F32)
    zn = jnp.zeros((T, MLA_NOPE_DIM), F32)
    zp = jnp.zeros((T, QK_PAD - MLA_NOPE_DIM - MLA_ROPE_DIM), F32)
    q_scale = MLA_SCALE * math.log2(math.e)
    cosq = jnp.concatenate([one, cos, zp], axis=1) * q_scale
    sinq = jnp.concatenate([zn, sin, zp], axis=1) * q_scale
    cs = jnp.concatenate([cos, sin, jnp.zeros((T, 128 - 2 * MLA_ROPE_DIM), F32)], axis=1)
    return cosq, sinq, cs


def _pad_heads(w, src_width, src_off, n, dst_off):
    K = w.shape[0]
    w = w.reshape(K, MLA_HEADS, src_width)[:, :, src_off:src_off + n]
    out = jnp.zeros((K, MLA_HEADS, QK_PAD), w.dtype)
    out = out.at[:, :, dst_off:dst_off + n].set(w)
    return out.reshape(K, MLA_HEADS * QK_PAD)


def kernel(x, ln_mix_g, w_in, shift_mu_prev, shift_mu_next, decay_w0_fwd, decay_w2_fwd, decay_w0_bwd, decay_w2_bwd, iclr_a0_fwd, iclr_a2_fwd, iclr_a0_bwd, iclr_a2_bwd, gate_g2, k_k, k_a, r_k, ln_x_g, ln_x_b, q_norm_g, w_uq, kv_norm_g, w_ukv, mla_out_g, w_out, ln_ffn_g, w_ffn_up, ffn_conv_w, ffn_conv_b, w_ffn_down, ln_final_g):
    B, T, D = x.shape
    assert w_in.shape[0] == 1, "single-layer block"
    n_tok = B * T
    row = lambda a: a.reshape(1, -1).astype(F32)
    l = 0

    w_in_l = w_in[l]
    w_r = w_in_l[:, :RWKV_COLS].astype(BF16)
    w_kr = w_in_l[:, RWKV_COLS + Q_LORA_RANK + KV_LORA_RANK:]
    w_m = jnp.concatenate([w_in_l[:, RWKV_COLS:], _rot_cols(w_kr),
                           jnp.zeros((D, MLA_COLS_PAD - (Q_LORA_RANK + KV_LORA_RANK + 2 * MLA_ROPE_DIM)), F32)],
                          axis=1).astype(BF16)
    ones_bd = _head_ones()
    prep_params = [
        row(shift_mu_prev[l]), row(shift_mu_next[l]),
        row(jnp.concatenate([decay_w0_fwd[l], decay_w0_bwd[l]])),
        _block_diag2(decay_w2_fwd[l], decay_w2_bwd[l]).astype(BF16),
        row(jnp.concatenate([iclr_a0_fwd[l], iclr_a0_bwd[l]])),
        _block_diag2(iclr_a2_fwd[l], iclr_a2_bwd[l]).astype(BF16),
        gate_g2[l].astype(BF16),
        row(k_k[l]), row(k_a[l]), row(r_k[l]), ones_bd, *_shift_diff_matrices(min(TILE_RWKV_PREP, T)),
    ]
    qw = MLA_NOPE_DIM + MLA_ROPE_DIM
    w_uq_l = w_uq[l]
    w_uq_h = w_uq_l.reshape(Q_LORA_RANK, MLA_HEADS, qw)
    wq_all = jnp.concatenate([w_uq_h, _rot_cols(w_uq_h[:, :, MLA_NOPE_DIM:])], axis=-1)
    wq_all = wq_all.reshape(Q_LORA_RANK, MLA_HEADS * QK_PAD).astype(BF16)
    kvw = MLA_NOPE_DIM + MLA_V_DIM
    w_ukv_l = w_ukv[l]
    wk_pad = _pad_heads(w_ukv_l, kvw, 0, MLA_NOPE_DIM, 0).astype(BF16)
    wv_t = w_ukv_l.reshape(KV_LORA_RANK, MLA_HEADS, kvw)[:, :, MLA_NOPE_DIM:].reshape(KV_LORA_RANK, D_MLA).T.astype(BF16)
    e2 = np.zeros((128, MLA_HEADS * QK_PAD), np.float32)
    for h in range(MLA_HEADS):
        for j in range(MLA_ROPE_DIM):
            e2[j, h * QK_PAD + MLA_NOPE_DIM + j] = 1.0
            e2[MLA_ROPE_DIM + j, h * QK_PAD + MLA_NOPE_DIM + j] = 1.0
    e2 = jnp.asarray(e2, dtype=BF16)
    cosq, sinq, cs = _rope_tables(T)
    mla_params = [row(q_norm_g[l]), row(kv_norm_g[l]), wq_all, wk_pad, wv_t, e2]
    w_out_l = w_out[l]
    mix_params = [row(ln_x_g[l]), row(ln_x_b[l]), ones_bd,
                  w_out_l[:D_RWKV].astype(BF16), w_out_l[D_RWKV:].astype(BF16), row(ln_ffn_g[l])]
    w_up = w_ffn_up[l]
    conv_w = ffn_conv_w[l]
    conv_b = ffn_conv_b[l]

    x2 = x.reshape(n_tok, D)
    z_r, z_m = _inproj(x2, row(ln_mix_g[l]), w_r, w_m)
    z_r = z_r.reshape(B, T, RWKV_COLS)
    z_m = z_m.reshape(B, T, MLA_COLS_PAD)

    r, k, v, kk, a_f, a_b, lw_f, lw_b, bonus, g = _rwkv_prep(z_r, prep_params)
    y_f, y_b = _wkv(r, k, v, kk, a_f, a_b, lw_f, lw_b, row(k_a[l]))

    q, kf, vt_mla = _mla_prep(z_m, mla_params, [cosq, sinq, cs])
    y_mla = _attn(q, kf, vt_mla, mla_out_g[l].reshape(D_MLA, 1).astype(F32))

    flat = lambda a: a.reshape(n_tok, a.shape[-1])
    h1, n2 = _mix_out(flat(y_f), flat(y_b), flat(bonus), flat(g), flat(y_mla), x2, mix_params)

    return _ffn(n2.reshape(B, T, D), h1.reshape(B, T, D),
                w_up[:, :D_FF].astype(BF16), w_up[:, D_FF:].astype(BF16),
                conv_w[:, :D_FF], conv_w[:, D_FF:], row(conv_b[:D_FF]), row(conv_b[D_FF:]),
                w_ffn_down[l].astype(BF16), row(ln_final_g))
```

```python
import functools
import math

import numpy as np
import jax
import jax.numpy as jnp
from jax import lax
from jax.experimental import pallas as pl
from jax.experimental.pallas import tpu as pltpu

F32 = jnp.float32
BF16 = jnp.bfloat16

D_MODEL = 1024
RWKV_HEADS = 8
HEAD_DIM = 64
D_RWKV = RWKV_HEADS * HEAD_DIM
DECAY_LORA = 64
ICLR_LORA = 64
GATE_LORA = 128
GN_EPS = 64e-5
L2_EPS = 1e-12
MLA_HEADS = 8
MLA_NOPE_DIM = 64
MLA_ROPE_DIM = 32
MLA_V_DIM = 64
D_MLA = MLA_HEADS * MLA_V_DIM
Q_LORA_RANK = 768
KV_LORA_RANK = 256
ROPE_THETA = 10000.0
MLA_SCALE = (MLA_NOPE_DIM + MLA_ROPE_DIM) ** -0.5
RWKV_COLS = 3 * D_RWKV + 2 * DECAY_LORA + 2 * ICLR_LORA + GATE_LORA
MLA_COLS_PAD = 1152
D_FF = 2816
NORM_EPS = 1e-6
QK_PAD = 128
VMEM_LIMIT = 56 * 1024 * 1024
BF16_SUBLANES = 16

WKV_CHUNK = 64
DECAY_LOG_SCALE = -math.exp(-0.5)

TILE_INPROJ = 1024
TILE_RWKV_PREP = 256
TILE_WKV = 512
WKV_HEADS_PER_STEP = 8
WKV_BATCH_PER_STEP = 2
TILE_MLA_PREP = 1024
TILE_ATTN_Q = 512
TILE_MIX_OUT = 1024
TILE_FFN = 512
FFN_COL_CHUNK = 256
FFN_HALO = BF16_SUBLANES


def _dot(a, b):
    return jnp.dot(a.astype(BF16), b.astype(BF16), preferred_element_type=F32)


def _dot_nt(a, b):
    return lax.dot_general(a.astype(BF16), b.astype(BF16), (((1,), (1,)), ((), ())),
                           preferred_element_type=F32)


def _sigmoid(x):
    return 1.0 / (1.0 + jnp.exp(-x))


def _rms(x, g):
    return x * lax.rsqrt(jnp.mean(x * x, axis=-1, keepdims=True) + NORM_EPS) * g


def _inproj_kernel(x_ref, g_ref, wr_ref, wm_ref, zr_ref, zm_ref):
    n = _rms(x_ref[...], g_ref[...]).astype(BF16)
    zr_ref[...] = jnp.dot(n, wr_ref[...], preferred_element_type=F32).astype(zr_ref.dtype)
    zm_ref[...] = jnp.dot(n, wm_ref[...], preferred_element_type=F32).astype(zm_ref.dtype)


def _inproj(x2, g, w_r, w_m):
    n_tok = x2.shape[0]
    tm = min(TILE_INPROJ, n_tok)
    full = lambda a: pl.BlockSpec(a.shape, lambda i: (0,) * a.ndim)
    return pl.pallas_call(
        _inproj_kernel,
        grid=(n_tok // tm,),
        in_specs=[pl.BlockSpec((tm, D_MODEL), lambda i: (i, 0)), full(g), full(w_r), full(w_m)],
        out_specs=[pl.BlockSpec((tm, RWKV_COLS), lambda i: (i, 0)),
                   pl.BlockSpec((tm, MLA_COLS_PAD), lambda i: (i, 0))],
        out_shape=[jax.ShapeDtypeStruct((n_tok, RWKV_COLS), BF16),
                   jax.ShapeDtypeStruct((n_tok, MLA_COLS_PAD), BF16)],
        compiler_params=pltpu.CompilerParams(dimension_semantics=("parallel",),
                                             vmem_limit_bytes=VMEM_LIMIT),
        name="inproj",
    )(x2, g, w_r, w_m)


def _rwkv_prep_kernel(z_ref, zp_ref, zn_ref, mup_ref, mun_ref, w0_ref, w2_ref, a0_ref, a2_ref,
                      g2_ref, kk_ref, ka_ref, rk_ref, ones_ref, dprev_ref, dnext_ref,
                      r_out, k_out, v_out, kkn_out, af_out, ab_out, lwf_out, lwb_out,
                      bonus_out, g_out):
    i = pl.program_id(1)
    nt = pl.num_programs(1)
    z16 = z_ref[0]
    z = z16.astype(F32)
    tt = z.shape[0]
    has_prev = (i > 0).astype(F32)
    has_next = (i < nt - 1).astype(F32)
    prev_row = zp_ref[0][BF16_SUBLANES - 1:BF16_SUBLANES, :].astype(F32) * has_prev
    next_row = zn_ref[0][0:1, :].astype(F32) * has_next
    d_prev = jnp.dot(dprev_ref[...], z16, preferred_element_type=F32)
    d_next = jnp.dot(dnext_ref[...], z16, preferred_element_type=F32)
    row8 = lax.broadcasted_iota(jnp.int32, (8, 1), 0)
    d_prev = jnp.concatenate([d_prev[:8] + jnp.where(row8 == 0, prev_row, 0.0), d_prev[8:]], axis=0)
    d_next = jnp.concatenate([d_next[:tt - 8], d_next[tt - 8:] + jnp.where(row8 == 7, next_row, 0.0)], axis=0)
    zs = z + mup_ref[...] * d_prev + mun_ref[...] * d_next

    d = D_RWKV
    r = zs[:, 0:d]
    k = zs[:, d:2 * d]
    v = zs[:, 2 * d:3 * d]
    wd = zs[:, 3 * d:3 * d + 128]
    ad = zs[:, 3 * d + 128:3 * d + 256]
    gd = zs[:, 3 * d + 256:3 * d + 384]

    logit = w0_ref[...] + _dot(jnp.tanh(wd), w2_ref[...])
    logw = DECAY_LOG_SCALE * _sigmoid(logit)
    a = _sigmoid(a0_ref[...] + _dot(ad, a2_ref[...]))
    g = _dot(_sigmoid(gd), g2_ref[...])
    a_f = a[:, :d]
    a_b = a[:, d:]

    ones_bd = ones_ref[...]
    kk = k * kk_ref[...]
    ss = _dot(kk * kk, ones_bd)
    kk = kk * lax.rsqrt(jnp.maximum(ss, L2_EPS * L2_EPS))
    k_a = ka_ref[...]
    k_fb = k * (2.0 + (a_f + a_b - 2.0) * k_a)
    bonus = _dot(r * k_fb * rk_ref[...], ones_bd) * v

    r_out[0] = r.astype(r_out.dtype)
    k_out[0] = k.astype(k_out.dtype)
    v_out[0] = v.astype(v_out.dtype)
    kkn_out[0] = kk.astype(kkn_out.dtype)
    af_out[0] = a_f.astype(af_out.dtype)
    ab_out[0] = a_b.astype(ab_out.dtype)
    lwf_out[0] = logw[:, :d]
    lwb_out[0] = logw[:, d:]
    bonus_out[0] = bonus.astype(bonus_out.dtype)
    g_out[0] = g.astype(g_out.dtype)


def _rwkv_prep(z_r, params):
    B, T, _ = z_r.shape
    tt = min(TILE_RWKV_PREP, T)
    nt = T // tt
    hr = BF16_SUBLANES
    hb = tt // hr
    full = lambda a: pl.BlockSpec(a.shape, lambda b, i: (0,) * a.ndim)
    seq = lambda w: pl.BlockSpec((1, tt, w), lambda b, i: (b, i, 0))
    in_specs = [
        seq(RWKV_COLS),
        pl.BlockSpec((1, hr, RWKV_COLS), lambda b, i: (b, jnp.maximum(i * hb - 1, 0), 0)),
        pl.BlockSpec((1, hr, RWKV_COLS), lambda b, i: (b, jnp.minimum((i + 1) * hb, T // hr - 1), 0)),
    ] + [full(p) for p in params]
    seq16 = jax.ShapeDtypeStruct((B, T, D_RWKV), BF16)
    seq32 = jax.ShapeDtypeStruct((B, T, D_RWKV), F32)
    return pl.pallas_call(
        _rwkv_prep_kernel,
        grid=(B, nt),
        in_specs=in_specs,
        out_specs=[seq(D_RWKV)] * 10,
        out_shape=[seq16, seq16, seq16, seq16, seq16, seq16, seq32, seq32, seq16, seq16],
        compiler_params=pltpu.CompilerParams(dimension_semantics=("parallel", "parallel"),
                                             vmem_limit_bytes=VMEM_LIMIT),
        name="rwkv_prep",
    )(z_r, z_r, z_r, *params)


def _wkv_operands(bi, rows, reverse, r_ref, k_ref, v_ref, kk_ref, a_ref, lw_ref, ka_ref):
    C = WKV_CHUNK
    lw = lw_ref[bi, rows, :]
    ti = lax.broadcasted_iota(jnp.int32, (C, C), 0)
    si = lax.broadcasted_iota(jnp.int32, (C, C), 1)
    if reverse:
        incl = si >= ti
        strict = si > ti
    else:
        incl = si <= ti
        strict = si < ti
    tri = incl.astype(BF16)
    lw_hi = lw.astype(BF16)
    lw_lo = (lw - lw_hi.astype(F32)).astype(BF16)
    cum = (jnp.dot(tri, lw_hi, preferred_element_type=F32)
           + jnp.dot(tri, lw_lo, preferred_element_type=F32))
    e_inc = jnp.exp(cum)
    e_dec = jnp.exp(-cum)
    e_exc = jnp.exp(cum - lw)
    r = r_ref[bi, rows, :].astype(F32)
    k = k_ref[bi, rows, :].astype(F32)
    kk = kk_ref[bi, rows, :].astype(F32)
    a = a_ref[bi, rows, :].astype(F32)
    k_dir = k * (1.0 + (a - 1.0) * ka_ref[...])
    t2 = lax.broadcasted_iota(jnp.int32, (C, 2 * C), 0)
    l2 = lax.broadcasted_iota(jnp.int32, (C, 2 * C), 1)
    s2 = jnp.bitwise_and(l2, C - 1)
    hi = l2 >= C
    if reverse:
        incl2 = s2 >= t2
        strict2 = s2 > t2
    else:
        incl2 = s2 <= t2
        strict2 = s2 < t2
    return dict(
        incl2=incl2, strict2=strict2, lo=jnp.logical_not(hi), hi=hi,
        sign_lo_neg=jnp.where(hi, 1.0, -1.0).astype(F32), eye2=(s2 == t2).astype(F32),
        p_end=jnp.exp(jnp.sum(lw, axis=0, keepdims=True)),
        alpha=(e_exc * kk).astype(BF16), beta=(e_dec * kk * a).astype(BF16),
        kappa=(e_dec * k_dir).astype(BF16), rho=(e_inc * r).astype(BF16),
        v=v_ref[bi, rows, :])


def _wkv_step(dirs, n_pairs):
    C = WKV_CHUNK
    W = 2 * C
    cat = jnp.concatenate
    zc = jnp.zeros((C, W), BF16)
    lo2 = lax.broadcasted_iota(jnp.int32, (W, W), 1) < C

    def keep(mask, a):
        return jnp.where(mask, a, jnp.zeros_like(a))

    def bd(a, lo, hi):
        return cat([keep(lo, a), keep(hi, a)], axis=0)

    def bd2(a_e, a_o):
        return cat([cat([a_e, zc], axis=1), cat([zc, a_o], axis=1)], axis=0)

    chains = []
    for d in dirs:
        ops = d["ops"]
        for pi in range(n_pairs):
            sl = slice(pi * W, (pi + 1) * W)
            v = ops["v"][:, sl]
            vt = cat([keep(ops["hi"], v), keep(ops["lo"], v)], axis=0).astype(F32).T.astype(BF16)
            chains.append(dict(d=d, pi=pi, sl=sl, al=ops["alpha"][:, sl], be=ops["beta"][:, sl],
                               ka=ops["kappa"][:, sl], rh=ops["rho"][:, sl], v=v,
                               vt_e=vt[:C], vt_o=vt[C:]))
    for c in chains:
        ar = cat([c["al"], c["rh"]], axis=0)
        rhs = cat([keep(lo2, cat([c["be"], c["ka"]], axis=0)),
                   keep(jnp.logical_not(lo2), cat([c["ka"], c["be"]], axis=0))], axis=0)
        g = _dot_nt(ar, rhs)
        c["g_e"], c["g_o"] = g[:, :W], g[:, W:]
    for c in chains:
        ops = c["d"]["ops"]
        lo, hi, strict2, incl2 = ops["lo"], ops["hi"], ops["strict2"], ops["incl2"]
        ga_e, gr_e = c["g_e"][:C], c["g_e"][C:]
        ga_o, gr_o = c["g_o"][:C], c["g_o"][C:]
        c["m"] = jnp.where(strict2, jnp.where(lo, -ga_e, -ga_o), 0.0).astype(BF16)
        ak_e = jnp.where(strict2, ga_e, 0.0).astype(BF16)
        ak_o = jnp.where(strict2, ga_o, 0.0).astype(BF16)
        c["wt_rhs"] = bd2(jnp.where(lo, c["al"], ak_e), jnp.where(hi, c["al"], ak_o))
        rbk_e = jnp.where(incl2, gr_e, 0.0) * ops["sign_lo_neg"]
        rbk_o = jnp.where(incl2, gr_o, 0.0) * (-ops["sign_lo_neg"])
        c["rbk"] = cat([rbk_e, rbk_o], axis=1).astype(BF16)
        c["P"] = ops["eye2"] + c["m"].astype(F32)
    for c in chains:
        ops = c["d"]["ops"]
        c["mp"] = jnp.dot(c["m"], bd(c["m"], ops["lo"], ops["hi"]), preferred_element_type=F32)
    n_sq = int(math.log2(C)) - 1
    for s in range(n_sq):
        last = s == n_sq - 1
        for c in chains:
            ops = c["d"]["ops"]
            mp = c["mp"].astype(BF16)
            p16 = c["P"].astype(BF16)
            out = jnp.dot(p16 if last else cat([p16, mp], axis=0), bd(mp, ops["lo"], ops["hi"]),
                          preferred_element_type=F32)
            c["P"] = c["P"] + out[:C]
            if not last:
                c["mp"] = out[C:]
    for c in chains:
        wt = _dot(c["P"], c["wt_rhs"]).astype(BF16)
        c["wt_e"], c["wt_o"] = wt[:, :W], wt[:, W:]
    for c in chains:
        ops = c["d"]["ops"]
        lo, hi = ops["lo"], ops["hi"]
        c["S"] = c["d"]["s_ref"][c["d"]["bi"], c["pi"]]
        s16 = c["S"].astype(BF16)
        sv_e = keep(lo, s16) + c["vt_e"]
        sv_o = keep(hi, s16) + c["vt_o"]
        lhs = cat([cat([c["wt_e"], keep(lo, c["rh"])], axis=0),
                   cat([c["wt_o"], keep(hi, c["rh"])], axis=0)], axis=1)
        c["x"] = _dot_nt(lhs, bd2(sv_e, sv_o))
        c["ut"] = _dot_nt(cat([sv_e, sv_o], axis=1), bd2(c["wt_e"], c["wt_o"]))
    for c in chains:
        ops = c["d"]["ops"]
        lo, hi = ops["lo"], ops["hi"]
        u = c["x"][:C].astype(BF16)
        vu = cat([keep(lo, u), keep(lo, c["v"]), keep(hi, c["v"]), keep(hi, u)], axis=0)
        c["y"] = c["x"][C:] + jnp.dot(c["rbk"], vu, preferred_element_type=F32)
        ut = c["ut"].astype(BF16)
        lhs = cat([c["vt_e"] - keep(lo, ut), c["vt_o"] - keep(hi, ut)], axis=1)
        kb = cat([keep(lo, c["be"]), keep(lo, c["ka"]), keep(hi, c["ka"]), keep(hi, c["be"])], axis=0)
        c["ds"] = jnp.dot(lhs, kb, preferred_element_type=F32)
    for c in chains:
        c["d"]["s_ref"][c["d"]["bi"], c["pi"]] = (c["S"] + c["ds"]) * c["d"]["ops"]["p_end"][:, c["sl"]]
    for d in dirs:
        d["y_ref"][d["bi"], d["rows"], :] =cat([c["y"] for c in chains if c["d"] is d], axis=1).astype(d["y_ref"].dtype)


def _wkv_kernel(rf_ref, kf_ref, vf_ref, kkf_ref, af_ref, lwf_ref,
                rb_ref, kb_ref, vb_ref, kkb_ref, ab_ref, lwb_ref, ka_ref,
                yf_ref, yb_ref, sf_ref, sb_ref, *, n_batch, n_heads, n_chunks):
    @pl.when(pl.program_id(2) == 0)
    def _():
        sf_ref[...] = jnp.zeros_like(sf_ref)
        sb_ref[...] = jnp.zeros_like(sb_ref)

    C = WKV_CHUNK

    def body(j, carry):
        jb = n_chunks - 1 - j
        rows_f = pl.ds(pl.multiple_of(j * C, C), C)
        rows_b = pl.ds(pl.multiple_of(jb * C, C), C)
        dirs = []
        for bi in range(n_batch):
            dirs.append(dict(bi=bi, rows=rows_f, y_ref=yf_ref, s_ref=sf_ref,
                             ops=_wkv_operands(bi, rows_f, False, rf_ref, kf_ref, vf_ref, kkf_ref,
                                               af_ref, lwf_ref, ka_ref)))
            dirs.append(dict(bi=bi, rows=rows_b, y_ref=yb_ref, s_ref=sb_ref,
                             ops=_wkv_operands(bi, rows_b, True, rb_ref, kb_ref, vb_ref, kkb_ref,
                                               ab_ref, lwb_ref, ka_ref)))
        _wkv_step(dirs, n_heads // 2)
        return carry

    lax.fori_loop(0, n_chunks, body, 0, unroll=2)


def _wkv(r, k, v, kk, a_f, a_b, lw_f, lw_b, k_a):
    B, T, _ = r.shape
    tt = min(TILE_WKV, T)
    hg = WKV_HEADS_PER_STEP
    nb = WKV_BATCH_PER_STEP if B % WKV_BATCH_PER_STEP == 0 else 1
    nt = T // tt
    ng = RWKV_HEADS // hg
    w = hg * HEAD_DIM
    cpt = tt // WKV_CHUNK
    fwd = pl.BlockSpec((nb, tt, w), lambda b, g, i: (b, i, g))
    bwd = pl.BlockSpec((nb, tt, w), lambda b, g, i: (b, nt - 1 - i, g))
    ka_spec = pl.BlockSpec((1, w), lambda b, g, i: (0, g))
    kern = functools.partial(_wkv_kernel, n_batch=nb, n_heads=hg, n_chunks=cpt)
    return pl.pallas_call(
        kern,
        grid=(B // nb, ng, nt),
        in_specs=[fwd] * 6 + [bwd] * 6 + [ka_spec],
        out_specs=[fwd, bwd],
        out_shape=[jax.ShapeDtypeStruct((B, T, D_RWKV), BF16)] * 2,
        scratch_shapes=[pltpu.VMEM((nb, hg // 2, HEAD_DIM, 2 * HEAD_DIM), F32)] * 2,
        compiler_params=pltpu.CompilerParams(
            dimension_semantics=("parallel", "parallel", "arbitrary"),
            vmem_limit_bytes=VMEM_LIMIT),
        name="wkv",
    )(r, k, v, kk, a_f, lw_f, r, k, v, kk, a_b, lw_b, k_a)


def _mla_prep_kernel(z_ref, qg_ref, kvg_ref, wq_ref, wk_ref, wvt_ref, e2_ref,
                     cosq_ref, sinq_ref, cs_ref, q_out, k_out, vt_out):
    z = z_ref[0].astype(F32)
    c_q = z[:, :Q_LORA_RANK]
    c_kv = z[:, Q_LORA_RANK:Q_LORA_RANK + KV_LORA_RANK]
    kr = z[:, Q_LORA_RANK + KV_LORA_RANK:]
    nq = _rms(c_q, qg_ref[...]).astype(BF16)
    nkv = _rms(c_kv, kvg_ref[...]).astype(BF16)
    q = jnp.dot(nq, wq_ref[...], preferred_element_type=F32)
    kf = (jnp.dot(nkv, wk_ref[...], preferred_element_type=F32)
          + _dot(kr * cs_ref[...], e2_ref[...]))
    vt = lax.dot_general(wvt_ref[...], nkv, (((1,), (1,)), ((), ())), preferred_element_type=F32)
    cosq = cosq_ref[...]
    sinq = sinq_ref[...]
    vt_out[0] = vt.astype(vt_out.dtype)
    for h in range(MLA_HEADS):
        qh = q[:, h * QK_PAD:(h + 1) * QK_PAD]
        q_rot = pltpu.roll(qh, QK_PAD - MLA_ROPE_DIM, 1)
        q_out[0, h] = (qh * cosq + q_rot * sinq).astype(q_out.dtype)
        k_out[0, h] = kf[:, h * QK_PAD:(h + 1) * QK_PAD].astype(k_out.dtype)


def _mla_prep(z_m, params, tabs):
    B, T, _ = z_m.shape
    tt = min(TILE_MLA_PREP, T)
    full = lambda a: pl.BlockSpec(a.shape, lambda b, i: (0,) * a.ndim)
    tab = lambda a: pl.BlockSpec((tt, a.shape[1]), lambda b, i: (i, 0))
    hspec = pl.BlockSpec((1, MLA_HEADS, tt, QK_PAD), lambda b, i: (b, 0, i, 0))
    vt_rows = D_MLA
    return pl.pallas_call(
        _mla_prep_kernel,
        grid=(B, T // tt),
        in_specs=[pl.BlockSpec((1, tt, MLA_COLS_PAD), lambda b, i: (b, i, 0))]
        + [full(p) for p in params] + [tab(t) for t in tabs],
        out_specs=[hspec, hspec, pl.BlockSpec((1, vt_rows, tt), lambda b, i: (b, 0, i))],
        out_shape=[jax.ShapeDtypeStruct((B, MLA_HEADS, T, QK_PAD), BF16),
                   jax.ShapeDtypeStruct((B, MLA_HEADS, T, QK_PAD), BF16),
                   jax.ShapeDtypeStruct((B, vt_rows, T), BF16)],
        compiler_params=pltpu.CompilerParams(dimension_semantics=("parallel", "parallel"),
                                             vmem_limit_bytes=VMEM_LIMIT),
        name="mla_prep",
    )(z_m, *params, *tabs)


def _attn_kernel(q_ref, k_ref, vt_ref, og_ref, o_ref):
    def scores(h):
        return _dot_nt(k_ref[0, h], q_ref[0, h])

    outs = []
    s_next = scores(0)
    for h in range(MLA_HEADS):
        s = s_next
        if h + 1 < MLA_HEADS:
            s_next = scores(h + 1)
        m = jnp.max(s, axis=0, keepdims=True)
        p = jnp.exp2(s - m)
        l = jnp.sum(p, axis=0, keepdims=True)
        o = jnp.dot(vt_ref[0, h * MLA_V_DIM:(h + 1) * MLA_V_DIM, :], p.astype(BF16),
                    preferred_element_type=F32)
        outs.append(o / l)
    o = jnp.concatenate(outs, axis=0)
    y = o * lax.rsqrt(jnp.mean(o * o, axis=0, keepdims=True) + NORM_EPS) * og_ref[...]
    o_ref[0] = y.T.astype(o_ref.dtype)


def _attn(q, k, vt, out_g_col):
    B, H, T, _ = q.shape
    tq = min(TILE_ATTN_Q, T)
    return pl.pallas_call(
        _attn_kernel,
        grid=(B, T // tq),
        in_specs=[pl.BlockSpec((1, H, tq, QK_PAD), lambda b, i: (b, 0, i, 0)),
                  pl.BlockSpec((1, H, T, QK_PAD), lambda b, i: (b, 0, 0, 0)),
                  pl.BlockSpec((1, vt.shape[1], T), lambda b, i: (b, 0, 0)),
                  pl.BlockSpec(out_g_col.shape, lambda b, i: (0, 0))],
        out_specs=pl.BlockSpec((1, tq, D_MLA), lambda b, i: (b, i, 0)),
        out_shape=jax.ShapeDtypeStruct((B, T, D_MLA), BF16),
        compiler_params=pltpu.CompilerParams(dimension_semantics=("parallel", "arbitrary"),
                                             vmem_limit_bytes=VMEM_LIMIT),
        name="attn",
    )(q, k, vt, out_g_col)


def _mix_out_kernel(yf_ref, yb_ref, bonus_ref, g_ref, ym_ref, x_ref, lng_ref, lnb_ref, ones_ref,
                    wo_r_ref, wo_m_ref, fg_ref, h_out, n_out):
    y = yf_ref[...].astype(F32) + yb_ref[...].astype(F32)
    ones_bd = ones_ref[...]
    inv_n = 1.0 / HEAD_DIM
    mu = _dot(y, ones_bd) * inv_n
    yc = y - mu
    var = _dot(yc * yc, ones_bd) * inv_n
    gn = yc * lax.rsqrt(var + GN_EPS) * lng_ref[...] + lnb_ref[...]
    y_rwkv = (gn + bonus_ref[...].astype(F32)) * g_ref[...].astype(F32)
    h = (x_ref[...] + _dot(y_rwkv, wo_r_ref[...])
         + jnp.dot(ym_ref[...], wo_m_ref[...], preferred_element_type=F32))
    h_out[...] = h
    n_out[...] = _rms(h, fg_ref[...]).astype(n_out.dtype)


def _mix_out(y_f, y_b, bonus, g, y_mla, x2, params):
    n_tok = x2.shape[0]
    tm = min(TILE_MIX_OUT, n_tok)
    full = lambda a: pl.BlockSpec(a.shape, lambda i: (0,) * a.ndim)
    tok = lambda w: pl.BlockSpec((tm, w), lambda i: (i, 0))
    return pl.pallas_call(
        _mix_out_kernel,
        grid=(n_tok // tm,),
        in_specs=[tok(D_RWKV)] * 4 + [tok(D_MLA), tok(D_MODEL)] + [full(p) for p in params],
        out_specs=[tok(D_MODEL), tok(D_MODEL)],
        out_shape=[jax.ShapeDtypeStruct((n_tok, D_MODEL), F32),
                   jax.ShapeDtypeStruct((n_tok, D_MODEL), BF16)],
        compiler_params=pltpu.CompilerParams(dimension_semantics=("parallel",),
                                             vmem_limit_bytes=VMEM_LIMIT),
        name="mix_out",
    )(y_f, y_b, bonus, g, y_mla, x2, *params)


def _ffn_kernel(n_ref, h_ref, wg_ref, wv_ref, cg_ref, cv_ref, bg_ref, bv_ref, wd_ref, fg_ref,
                o_ref, npad_ref, act_ref):
    i = pl.program_id(1)
    T = n_ref.shape[1]
    R = o_ref.shape[1]
    H = FFN_HALO

    @pl.when(i == 0)
    def _():
        zeros = jnp.zeros((H, D_MODEL), npad_ref.dtype)
        npad_ref[0:H, :] = zeros
        npad_ref[H + T:H + T + H, :] = zeros
        npad_ref[H:H + T, :] = n_ref[0]

    n_win = npad_ref[pl.ds(pl.multiple_of(i * R, R), R + 2 * H), :]

    def conv(u, c, b):
        return c[0:1] * u[H - 1:H - 1 + R] + c[1:2] * u[H:H + R] + c[2:3] * u[H + 1:H + 1 + R] + b

    fc = FFN_COL_CHUNK
    for j in range(D_FF // fc):
        cs = slice(j * fc, (j + 1) * fc)
        gate = conv(jnp.dot(n_win, wg_ref[:, cs], preferred_element_type=F32), cg_ref[:, cs], bg_ref[:, cs])
        val = conv(jnp.dot(n_win, wv_ref[:, cs], preferred_element_type=F32), cv_ref[:, cs], bv_ref[:, cs])
        act_ref[:, cs] = (gate * _sigmoid(gate) * val).astype(act_ref.dtype)
    f = jnp.dot(act_ref[...], wd_ref[...], preferred_element_type=F32)
    o_ref[0] = _rms(h_ref[0] + f, fg_ref[...])


def _ffn(n2, h1, w_up, conv_w, conv_b, w_down, final_g):
    B, T, _ = n2.shape
    R = min(TILE_FFN, T)
    once = lambda a: pl.BlockSpec(a.shape, lambda b, i: (0,) * a.ndim, pipeline_mode=pl.Buffered(1))
    half = lambda a, j: pl.BlockSpec((a.shape[0], D_FF), lambda b, i: (0, j), pipeline_mode=pl.Buffered(1))
    rows = pl.BlockSpec((1, R, D_MODEL), lambda b, i: (b, i, 0))
    return pl.pallas_call(
        _ffn_kernel,
        grid=(B, T // R),
        in_specs=[pl.BlockSpec((1, T, D_MODEL), lambda b, i: (b, 0, 0)), rows,
                  half(w_up, 0), half(w_up, 1), half(conv_w, 0), half(conv_w, 1),
                  half(conv_b, 0), half(conv_b, 1), once(w_down), once(final_g)],
        out_specs=rows,
        out_shape=jax.ShapeDtypeStruct((B, T, D_MODEL), F32),
        scratch_shapes=[pltpu.VMEM((T + 2 * FFN_HALO, D_MODEL), BF16),
                        pltpu.VMEM((R, D_FF), BF16)],
        compiler_params=pltpu.CompilerParams(dimension_semantics=("parallel", "arbitrary"),
                                             vmem_limit_bytes=VMEM_LIMIT),
        name="ffn",
    )(n2, h1, w_up, w_up, conv_w, conv_w, conv_b, conv_b, w_down, final_g)


def _rot_cols(w):
    half = w.shape[-1] // 2
    return jnp.concatenate([-w[..., half:], w[..., :half]], axis=-1)


def _block_diag2(a, b):
    za = jnp.zeros((a.shape[0], b.shape[1]), a.dtype)
    zb = jnp.zeros((b.shape[0], a.shape[1]), b.dtype)
    return jnp.concatenate([jnp.concatenate([a, za], axis=1), jnp.concatenate([zb, b], axis=1)], axis=0)


def _shift_diff_matrices(n):
    eye = np.eye(n, dtype=np.float32)
    d_prev = np.eye(n, k=-1, dtype=np.float32) - eye
    d_next = np.eye(n, k=1, dtype=np.float32) - eye
    return jnp.asarray(d_prev, dtype=BF16), jnp.asarray(d_next, dtype=BF16)


def _head_ones():
    idx = np.arange(D_RWKV) // HEAD_DIM
    return jnp.asarray((idx[:, None] == idx[None, :]).astype(np.float32), dtype=BF16)


def _rope_tables(T):
    inv_freq = jnp.power(ROPE_THETA, -jnp.arange(0, MLA_ROPE_DIM, 2, dtype=F32) / MLA_ROPE_DIM)
    ang = jnp.arange(T, dtype=F32)[:, None] * inv_freq[None, :]
    ang = jnp.concatenate([ang, ang], axis=-1)
    cos, sin = jnp.cos(ang), jnp.sin(ang)
    one = jnp.ones((T, MLA_NOPE_DIM), F32)
    zn = jnp.zeros((T, MLA_NOPE_DIM), F32)
    zp = jnp.zeros((T, QK_PAD - MLA_NOPE_DIM - MLA_ROPE_DIM), F32)
    q_scale = MLA_SCALE * math.log2(math.e)
    cosq = jnp.concatenate([one, cos, zp], axis=1) * q_scale
    sinq = jnp.concatenate([zn, sin, zp], axis=1) * q_scale
    cs = jnp.concatenate([cos, sin, jnp.zeros((T, 128 - 2 * MLA_ROPE_DIM), F32)], axis=1)
    return cosq, sinq, cs


def _pad_heads(w, src_width, src_off, n, dst_off):
    K = w.shape[0]
    w = w.reshape(K, MLA_HEADS, src_width)[:, :, src_off:src_off + n]
    out = jnp.zeros((K, MLA_HEADS, QK_PAD), w.dtype)
    out = out.at[:, :, dst_off:dst_off + n].set(w)
    return out.reshape(K, MLA_HEADS * QK_PAD)


def kernel(x, ln_mix_g, w_in, shift_mu_prev, shift_mu_next, decay_w0_fwd, decay_w2_fwd, decay_w0_bwd, decay_w2_bwd, iclr_a0_fwd, iclr_a2_fwd, iclr_a0_bwd, iclr_a2_bwd, gate_g2, k_k, k_a, r_k, ln_x_g, ln_x_b, q_norm_g, w_uq, kv_norm_g, w_ukv, mla_out_g, w_out, ln_ffn_g, w_ffn_up, ffn_conv_w, ffn_conv_b, w_ffn_down, ln_final_g):
    B, T, D = x.shape
    assert w_in.shape[0] == 1, "single-layer block"
    n_tok = B * T
    row = lambda a: a.reshape(1, -1).astype(F32)
    l = 0

    w_in_l = w_in[l]
    w_r = w_in_l[:, :RWKV_COLS].astype(BF16)
    w_kr = w_in_l[:, RWKV_COLS + Q_LORA_RANK + KV_LORA_RANK:]
    w_m = jnp.concatenate([w_in_l[:, RWKV_COLS:], _rot_cols(w_kr),
                           jnp.zeros((D, MLA_COLS_PAD - (Q_LORA_RANK + KV_LORA_RANK + 2 * MLA_ROPE_DIM)), F32)],
                          axis=1).astype(BF16)
    ones_bd = _head_ones()
    prep_params = [
        row(shift_mu_prev[l]), row(shift_mu_next[l]),
        row(jnp.concatenate([decay_w0_fwd[l], decay_w0_bwd[l]])),
        _block_diag2(decay_w2_fwd[l], decay_w2_bwd[l]).astype(BF16),
        row(jnp.concatenate([iclr_a0_fwd[l], iclr_a0_bwd[l]])),
        _block_diag2(iclr_a2_fwd[l], iclr_a2_bwd[l]).astype(BF16),
        gate_g2[l].astype(BF16),
        row(k_k[l]), row(k_a[l]), row(r_k[l]), ones_bd, *_shift_diff_matrices(min(TILE_RWKV_PREP, T)),
    ]
    qw = MLA_NOPE_DIM + MLA_ROPE_DIM
    w_uq_l = w_uq[l]
    w_uq_h = w_uq_l.reshape(Q_LORA_RANK, MLA_HEADS, qw)
    wq_all = jnp.concatenate([w_uq_h, _rot_cols(w_uq_h[:, :, MLA_NOPE_DIM:])], axis=-1)
    wq_all = wq_all.reshape(Q_LORA_RANK, MLA_HEADS * QK_PAD).astype(BF16)
    kvw = MLA_NOPE_DIM + MLA_V_DIM
    w_ukv_l = w_ukv[l]
    wk_pad = _pad_heads(w_ukv_l, kvw, 0, MLA_NOPE_DIM, 0).astype(BF16)
    wv_t = w_ukv_l.reshape(KV_LORA_RANK, MLA_HEADS, kvw)[:, :, MLA_NOPE_DIM:].reshape(KV_LORA_RANK, D_MLA).T.astype(BF16)
    e2 = np.zeros((128, MLA_HEADS * QK_PAD), np.float32)
    for h in range(MLA_HEADS):
        for j in range(MLA_ROPE_DIM):
            e2[j, h * QK_PAD + MLA_NOPE_DIM + j] = 1.0
            e2[MLA_ROPE_DIM + j, h * QK_PAD + MLA_NOPE_DIM + j] = 1.0
    e2 = jnp.asarray(e2, dtype=BF16)
    cosq, sinq, cs = _rope_tables(T)
    mla_params = [row(q_norm_g[l]), row(kv_norm_g[l]), wq_all, wk_pad, wv_t, e2]
    w_out_l = w_out[l]
    mix_params = [row(ln_x_g[l]), row(ln_x_b[l]), ones_bd,
                  w_out_l[:D_RWKV].astype(BF16), w_out_l[D_RWKV:].astype(BF16), row(ln_ffn_g[l])]
    w_up = w_ffn_up[l]
    conv_w = ffn_conv_w[l]
    conv_b = ffn_conv_b[l]

    x2 = x.reshape(n_tok, D)
    z_r, z_m = _inproj(x2, row(ln_mix_g[l]), w_r, w_m)
    z_r = z_r.reshape(B, T, RWKV_COLS)
    z_m = z_m.reshape(B, T, MLA_COLS_PAD)

    r, k, v, kk, a_f, a_b, lw_f, lw_b, bonus, g = _rwkv_prep(z_r, prep_params)
    y_f, y_b = _wkv(r, k, v, kk, a_f, a_b, lw_f, lw_b, row(k_a[l]))

    q, kf, vt_mla = _mla_prep(z_m, mla_params, [cosq, sinq, cs])
    y_mla = _attn(q, kf, vt_mla, mla_out_g[l].reshape(D_MLA, 1).astype(F32))

    flat = lambda a: a.reshape(n_tok, a.shape[-1])
    h1, n2 = _mix_out(flat(y_f), flat(y_b), flat(bonus), flat(g), flat(y_mla), x2, mix_params)

    return _ffn(n2.reshape(B, T, D), h1.reshape(B, T, D),
                w_up.astype(BF16), conv_w, row(conv_b), w_ffn_down[l].astype(BF16), row(ln_final_g))
```

```python
import functools
import math

import numpy as np
import jax
import jax.numpy as jnp
from jax import lax
from jax.experimental import pallas as pl
from jax.experimental.pallas import tpu as pltpu

F32 = jnp.float32
BF16 = jnp.bfloat16

D_MODEL = 1024
RWKV_HEADS = 8
HEAD_DIM = 64
D_RWKV = RWKV_HEADS * HEAD_DIM
DECAY_LORA = 64
ICLR_LORA = 64
GATE_LORA = 128
GN_EPS = 64e-5
L2_EPS = 1e-12
MLA_HEADS = 8
MLA_NOPE_DIM = 64
MLA_ROPE_DIM = 32
MLA_V_DIM = 64
D_MLA = MLA_HEADS * MLA_V_DIM
Q_LORA_RANK = 768
KV_LORA_RANK = 256
ROPE_THETA = 10000.0
MLA_SCALE = (MLA_NOPE_DIM + MLA_ROPE_DIM) ** -0.5
RWKV_COLS = 3 * D_RWKV + 2 * DECAY_LORA + 2 * ICLR_LORA + GATE_LORA
MLA_COLS_PAD = 1152
D_FF = 2816
NORM_EPS = 1e-6
QK_PAD = 128
VMEM_LIMIT = 56 * 1024 * 1024
BF16_SUBLANES = 16

WKV_CHUNK = 64
DECAY_LOG_SCALE = -math.exp(-0.5)

TILE_INPROJ = 1024
TILE_RWKV_PREP = 256
TILE_WKV = 512
WKV_HEADS_PER_STEP = 8
WKV_BATCH_PER_STEP = 2
WKV_UNROLL = 4
TILE_MLA_PREP = 1024
TILE_ATTN_Q = 512
TILE_MIX_OUT = 1024
TILE_FFN = 512
FFN_COL_CHUNK = 256
FFN_DOWN_GROUP = 6
FFN_HALO = BF16_SUBLANES


def _dot(a, b):
    return jnp.dot(a.astype(BF16), b.astype(BF16), preferred_element_type=F32)


def _dot_nt(a, b):
    return lax.dot_general(a.astype(BF16), b.astype(BF16), (((1,), (1,)), ((), ())),
                           preferred_element_type=F32)


def _sigmoid(x):
    return 1.0 / (1.0 + jnp.exp(-x))


def _rms(x, g):
    return x * lax.rsqrt(jnp.mean(x * x, axis=-1, keepdims=True) + NORM_EPS) * g


def _inproj_kernel(x_ref, g_ref, wr_ref, wm_ref, zr_ref, zm_ref):
    n = _rms(x_ref[...], g_ref[...]).astype(BF16)
    zr_ref[...] = jnp.dot(n, wr_ref[...], preferred_element_type=F32).astype(zr_ref.dtype)
    zm_ref[...] = jnp.dot(n, wm_ref[...], preferred_element_type=F32).astype(zm_ref.dtype)


def _inproj(x2, g, w_r, w_m):
    n_tok = x2.shape[0]
    tm = min(TILE_INPROJ, n_tok)
    full = lambda a: pl.BlockSpec(a.shape, lambda i: (0,) * a.ndim)
    return pl.pallas_call(
        _inproj_kernel,
        grid=(n_tok // tm,),
        in_specs=[pl.BlockSpec((tm, D_MODEL), lambda i: (i, 0)), full(g), full(w_r), full(w_m)],
        out_specs=[pl.BlockSpec((tm, RWKV_COLS), lambda i: (i, 0)),
                   pl.BlockSpec((tm, MLA_COLS_PAD), lambda i: (i, 0))],
        out_shape=[jax.ShapeDtypeStruct((n_tok, RWKV_COLS), BF16),
                   jax.ShapeDtypeStruct((n_tok, MLA_COLS_PAD), BF16)],
        compiler_params=pltpu.CompilerParams(dimension_semantics=("parallel",),
                                             vmem_limit_bytes=VMEM_LIMIT),
        name="inproj",
    )(x2, g, w_r, w_m)


def _rwkv_prep_kernel(z_ref, zp_ref, zn_ref, mup_ref, mun_ref, w0_ref, w2_ref, a0_ref, a2_ref,
                      g2_ref, kk_ref, ka_ref, rk_ref, ones_ref, dprev_ref, dnext_ref,
                      r_out, k_out, v_out, kkn_out, af_out, ab_out, lwf_out, lwb_out,
                      bonus_out, g_out):
    i = pl.program_id(1)
    nt = pl.num_programs(1)
    z16 = z_ref[0]
    z = z16.astype(F32)
    tt = z.shape[0]
    has_prev = (i > 0).astype(F32)
    has_next = (i < nt - 1).astype(F32)
    prev_row = zp_ref[0][BF16_SUBLANES - 1:BF16_SUBLANES, :].astype(F32) * has_prev
    next_row = zn_ref[0][0:1, :].astype(F32) * has_next
    d_prev = jnp.dot(dprev_ref[...], z16, preferred_element_type=F32)
    d_next = jnp.dot(dnext_ref[...], z16, preferred_element_type=F32)
    row8 = lax.broadcasted_iota(jnp.int32, (8, 1), 0)
    d_prev = jnp.concatenate([d_prev[:8] + jnp.where(row8 == 0, prev_row, 0.0), d_prev[8:]], axis=0)
    d_next = jnp.concatenate([d_next[:tt - 8], d_next[tt - 8:] + jnp.where(row8 == 7, next_row, 0.0)], axis=0)
    zs = z + mup_ref[...] * d_prev + mun_ref[...] * d_next

    d = D_RWKV
    r = zs[:, 0:d]
    k = zs[:, d:2 * d]
    v = zs[:, 2 * d:3 * d]
    wd = zs[:, 3 * d:3 * d + 128]
    ad = zs[:, 3 * d + 128:3 * d + 256]
    gd = zs[:, 3 * d + 256:3 * d + 384]

    logit = w0_ref[...] + _dot(jnp.tanh(wd), w2_ref[...])
    logw = DECAY_LOG_SCALE * _sigmoid(logit)
    a = _sigmoid(a0_ref[...] + _dot(ad, a2_ref[...]))
    g = _dot(_sigmoid(gd), g2_ref[...])
    a_f = a[:, :d]
    a_b = a[:, d:]

    ones_bd = ones_ref[...]
    kk = k * kk_ref[...]
    ss = _dot(kk * kk, ones_bd)
    kk = kk * lax.rsqrt(jnp.maximum(ss, L2_EPS * L2_EPS))
    k_a = ka_ref[...]
    k_fb = k * (2.0 + (a_f + a_b - 2.0) * k_a)
    bonus = _dot(r * k_fb * rk_ref[...], ones_bd) * v

    r_out[0] = r.astype(r_out.dtype)
    k_out[0] = k.astype(k_out.dtype)
    v_out[0] = v.astype(v_out.dtype)
    kkn_out[0] = kk.astype(kkn_out.dtype)
    af_out[0] = a_f.astype(af_out.dtype)
    ab_out[0] = a_b.astype(ab_out.dtype)
    lwf_out[0] = logw[:, :d]
    lwb_out[0] = logw[:, d:]
    bonus_out[0] = bonus.astype(bonus_out.dtype)
    g_out[0] = g.astype(g_out.dtype)


def _rwkv_prep(z_r, params):
    B, T, _ = z_r.shape
    tt = min(TILE_RWKV_PREP, T)
    nt = T // tt
    hr = BF16_SUBLANES
    hb = tt // hr
    full = lambda a: pl.BlockSpec(a.shape, lambda b, i: (0,) * a.ndim)
    seq = lambda w: pl.BlockSpec((1, tt, w), lambda b, i: (b, i, 0))
    in_specs = [
        seq(RWKV_COLS),
        pl.BlockSpec((1, hr, RWKV_COLS), lambda b, i: (b, jnp.maximum(i * hb - 1, 0), 0)),
        pl.BlockSpec((1, hr, RWKV_COLS), lambda b, i: (b, jnp.minimum((i + 1) * hb, T // hr - 1), 0)),
    ] + [full(p) for p in params]
    seq16 = jax.ShapeDtypeStruct((B, T, D_RWKV), BF16)
    seq32 = jax.ShapeDtypeStruct((B, T, D_RWKV), F32)
    return pl.pallas_call(
        _rwkv_prep_kernel,
        grid=(B, nt),
        in_specs=in_specs,
        out_specs=[seq(D_RWKV)] * 10,
        out_shape=[seq16, seq16, seq16, seq16, seq16, seq16, seq32, seq32, seq16, seq16],
        compiler_params=pltpu.CompilerParams(dimension_semantics=("parallel", "parallel"),
                                             vmem_limit_bytes=VMEM_LIMIT),
        name="rwkv_prep",
    )(z_r, z_r, z_r, *params)


def _wkv_operands(bi, rows, reverse, r_ref, k_ref, v_ref, kk_ref, a_ref, lw_ref, ka_ref):
    C = WKV_CHUNK
    lw = lw_ref[bi, rows, :]
    ti = lax.broadcasted_iota(jnp.int32, (C, C), 0)
    si = lax.broadcasted_iota(jnp.int32, (C, C), 1)
    if reverse:
        incl = si >= ti
        strict = si > ti
    else:
        incl = si <= ti
        strict = si < ti
    tri = incl.astype(BF16)
    lw_hi = lw.astype(BF16)
    lw_lo = (lw - lw_hi.astype(F32)).astype(BF16)
    cum = (jnp.dot(tri, lw_hi, preferred_element_type=F32)
           + jnp.dot(tri, lw_lo, preferred_element_type=F32))
    e_inc = jnp.exp(cum)
    e_dec = jnp.exp(-cum)
    e_exc = jnp.exp(cum - lw)
    r = r_ref[bi, rows, :].astype(F32)
    k = k_ref[bi, rows, :].astype(F32)
    kk = kk_ref[bi, rows, :].astype(F32)
    a = a_ref[bi, rows, :].astype(F32)
    k_dir = k * (1.0 + (a - 1.0) * ka_ref[...])
    t2 = lax.broadcasted_iota(jnp.int32, (C, 2 * C), 0)
    l2 = lax.broadcasted_iota(jnp.int32, (C, 2 * C), 1)
    s2 = jnp.bitwise_and(l2, C - 1)
    hi = l2 >= C
    if reverse:
        incl2 = s2 >= t2
        strict2 = s2 > t2
    else:
        incl2 = s2 <= t2
        strict2 = s2 < t2
    return dict(
        incl2=incl2, strict2=strict2, lo=jnp.logical_not(hi), hi=hi,
        sign_lo_neg=jnp.where(hi, 1.0, -1.0).astype(F32), eye2=(s2 == t2).astype(F32),
        p_end=jnp.exp(jnp.sum(lw, axis=0, keepdims=True)),
        alpha=(e_exc * kk).astype(BF16), beta=(e_dec * kk * a).astype(BF16),
        kappa=(e_dec * k_dir).astype(BF16), rho=(e_inc * r).astype(BF16),
        v=v_ref[bi, rows, :])


def _wkv_step(dirs, n_pairs):
    C = WKV_CHUNK
    W = 2 * C
    cat = jnp.concatenate
    zc = jnp.zeros((C, W), BF16)
    lo2 = lax.broadcasted_iota(jnp.int32, (W, W), 1) < C

    def keep(mask, a):
        return jnp.where(mask, a, jnp.zeros_like(a))

    def bd(a, lo, hi):
        return cat([keep(lo, a), keep(hi, a)], axis=0)

    def bd2(a_e, a_o):
        return cat([cat([a_e, zc], axis=1), cat([zc, a_o], axis=1)], axis=0)

    chains = []
    for d in dirs:
        ops = d["ops"]
        for pi in range(n_pairs):
            sl = slice(pi * W, (pi + 1) * W)
            v = ops["v"][:, sl]
            vt = cat([keep(ops["hi"], v), keep(ops["lo"], v)], axis=0).astype(F32).T.astype(BF16)
            chains.append(dict(d=d, pi=pi, sl=sl, al=ops["alpha"][:, sl], be=ops["beta"][:, sl],
                               ka=ops["kappa"][:, sl], rh=ops["rho"][:, sl], v=v,
                               vt_e=vt[:C], vt_o=vt[C:]))
    for c in chains:
        ar = cat([c["al"], c["rh"]], axis=0)
        rhs = cat([keep(lo2, cat([c["be"], c["ka"]], axis=0)),
                   keep(jnp.logical_not(lo2), cat([c["ka"], c["be"]], axis=0))], axis=0)
        g = _dot_nt(ar, rhs)
        c["g_e"], c["g_o"] = g[:, :W], g[:, W:]
    for c in chains:
        ops = c["d"]["ops"]
        lo, hi, strict2, incl2 = ops["lo"], ops["hi"], ops["strict2"], ops["incl2"]
        ga_e, gr_e = c["g_e"][:C], c["g_e"][C:]
        ga_o, gr_o = c["g_o"][:C], c["g_o"][C:]
        c["m"] = jnp.where(strict2, jnp.where(lo, -ga_e, -ga_o), 0.0).astype(BF16)
        ak_e = jnp.where(strict2, ga_e, 0.0).astype(BF16)
        ak_o = jnp.where(strict2, ga_o, 0.0).astype(BF16)
        c["wt_rhs"] = bd2(jnp.where(lo, c["al"], ak_e), jnp.where(hi, c["al"], ak_o))
        rbk_e = jnp.where(incl2, gr_e, 0.0) * ops["sign_lo_neg"]
        rbk_o = jnp.where(incl2, gr_o, 0.0) * (-ops["sign_lo_neg"])
        c["rbk"] = cat([rbk_e, rbk_o], axis=1).astype(BF16)
        c["P"] = ops["eye2"] + c["m"].astype(F32)
    for c in chains:
        ops = c["d"]["ops"]
        c["mp"] = jnp.dot(c["m"], bd(c["m"], ops["lo"], ops["hi"]), preferred_element_type=F32)
    n_sq = int(math.log2(C)) - 1
    for s in range(n_sq):
        last = s == n_sq - 1
        for c in chains:
            ops = c["d"]["ops"]
            mp = c["mp"].astype(BF16)
            p16 = c["P"].astype(BF16)
            out = jnp.dot(p16 if last else cat([p16, mp], axis=0), bd(mp, ops["lo"], ops["hi"]),
                          preferred_element_type=F32)
            c["P"] = c["P"] + out[:C]
            if not last:
                c["mp"] = out[C:]
    for c in chains:
        wt = _dot(c["P"], c["wt_rhs"]).astype(BF16)
        c["wt_e"], c["wt_o"] = wt[:, :W], wt[:, W:]
    for c in chains:
        ops = c["d"]["ops"]
        lo, hi = ops["lo"], ops["hi"]
        c["S"] = c["d"]["s_ref"][c["d"]["bi"], c["pi"]]
        s16 = c["S"].astype(BF16)
        sv_e = keep(lo, s16) + c["vt_e"]
        sv_o = keep(hi, s16) + c["vt_o"]
        lhs = cat([cat([c["wt_e"], keep(lo, c["rh"])], axis=0),
                   cat([c["wt_o"], keep(hi, c["rh"])], axis=0)], axis=1)
        c["x"] = _dot_nt(lhs, bd2(sv_e, sv_o))
        c["ut"] = _dot_nt(cat([sv_e, sv_o], axis=1), bd2(c["wt_e"], c["wt_o"]))
    for c in chains:
        ops = c["d"]["ops"]
        lo, hi = ops["lo"], ops["hi"]
        u = c["x"][:C].astype(BF16)
        vu = cat([keep(lo, u), keep(lo, c["v"]), keep(hi, c["v"]), keep(hi, u)], axis=0)
        c["y"] = c["x"][C:] + jnp.dot(c["rbk"], vu, preferred_element_type=F32)
        ut = c["ut"].astype(BF16)
        lhs = cat([c["vt_e"] - keep(lo, ut), c["vt_o"] - keep(hi, ut)], axis=1)
        kb = cat([keep(lo, c["be"]), keep(lo, c["ka"]), keep(hi, c["ka"]), keep(hi, c["be"])], axis=0)
        c["ds"] = jnp.dot(lhs, kb, preferred_element_type=F32)
    for c in chains:
        c["d"]["s_ref"][c["d"]["bi"], c["pi"]] = (c["S"] + c["ds"]) * c["d"]["ops"]["p_end"][:, c["sl"]]
    for d in dirs:
        d["y_ref"][d["bi"], d["rows"], :] =cat([c["y"] for c in chains if c["d"] is d], axis=1).astype(d["y_ref"].dtype)


def _wkv_kernel(rf_ref, kf_ref, vf_ref, kkf_ref, af_ref, lwf_ref,
                rb_ref, kb_ref, vb_ref, kkb_ref, ab_ref, lwb_ref, ka_ref,
                yf_ref, yb_ref, sf_ref, sb_ref, *, n_batch, n_heads, n_chunks):
    @pl.when(pl.program_id(2) == 0)
    def _():
        sf_ref[...] = jnp.zeros_like(sf_ref)
        sb_ref[...] = jnp.zeros_like(sb_ref)

    C = WKV_CHUNK

    def body(j, carry):
        jb = n_chunks - 1 - j
        rows_f = pl.ds(pl.multiple_of(j * C, C), C)
        rows_b = pl.ds(pl.multiple_of(jb * C, C), C)
        dirs = []
        for bi in range(n_batch):
            dirs.append(dict(bi=bi, rows=rows_f, y_ref=yf_ref, s_ref=sf_ref,
                             ops=_wkv_operands(bi, rows_f, False, rf_ref, kf_ref, vf_ref, kkf_ref,
                                               af_ref, lwf_ref, ka_ref)))
            dirs.append(dict(bi=bi, rows=rows_b, y_ref=yb_ref, s_ref=sb_ref,
                             ops=_wkv_operands(bi, rows_b, True, rb_ref, kb_ref, vb_ref, kkb_ref,
                                               ab_ref, lwb_ref, ka_ref)))
        _wkv_step(dirs, n_heads // 2)
        return carry

    lax.fori_loop(0, n_chunks, body, 0, unroll=min(WKV_UNROLL, n_chunks))


def _wkv(r, k, v, kk, a_f, a_b, lw_f, lw_b, k_a):
    B, T, _ = r.shape
    tt = min(TILE_WKV, T)
    hg = WKV_HEADS_PER_STEP
    nb = WKV_BATCH_PER_STEP if B % WKV_BATCH_PER_STEP == 0 else 1
    nt = T // tt
    ng = RWKV_HEADS // hg
    w = hg * HEAD_DIM
    cpt = tt // WKV_CHUNK
    fwd = pl.BlockSpec((nb, tt, w), lambda b, g, i: (b, i, g))
    bwd = pl.BlockSpec((nb, tt, w), lambda b, g, i: (b, nt - 1 - i, g))
    ka_spec = pl.BlockSpec((1, w), lambda b, g, i: (0, g))
    kern = functools.partial(_wkv_kernel, n_batch=nb, n_heads=hg, n_chunks=cpt)
    return pl.pallas_call(
        kern,
        grid=(B // nb, ng, nt),
        in_specs=[fwd] * 6 + [bwd] * 6 + [ka_spec],
        out_specs=[fwd, bwd],
        out_shape=[jax.ShapeDtypeStruct((B, T, D_RWKV), BF16)] * 2,
        scratch_shapes=[pltpu.VMEM((nb, hg // 2, HEAD_DIM, 2 * HEAD_DIM), F32)] * 2,
        compiler_params=pltpu.CompilerParams(
            dimension_semantics=("parallel", "parallel", "arbitrary"),
            vmem_limit_bytes=VMEM_LIMIT),
        name="wkv",
    )(r, k, v, kk, a_f, lw_f, r, k, v, kk, a_b, lw_b, k_a)


def _mla_prep_kernel(z_ref, qg_ref, kvg_ref, wq_ref, wk_ref, wvt_ref, e2_ref,
                     cosq_ref, sinq_ref, cs_ref, q_out, k_out, vt_out):
    z = z_ref[0].astype(F32)
    c_q = z[:, :Q_LORA_RANK]
    c_kv = z[:, Q_LORA_RANK:Q_LORA_RANK + KV_LORA_RANK]
    kr = z[:, Q_LORA_RANK + KV_LORA_RANK:]
    nq = _rms(c_q, qg_ref[...]).astype(BF16)
    nkv = _rms(c_kv, kvg_ref[...]).astype(BF16)
    q = jnp.dot(nq, wq_ref[...], preferred_element_type=F32)
    kf = (jnp.dot(nkv, wk_ref[...], preferred_element_type=F32)
          + _dot(kr * cs_ref[...], e2_ref[...]))
    vt = lax.dot_general(wvt_ref[...], nkv, (((1,), (1,)), ((), ())), preferred_element_type=F32)
    cosq = cosq_ref[...]
    sinq = sinq_ref[...]
    vt_out[0] = vt.astype(vt_out.dtype)
    for h in range(MLA_HEADS):
        qh = q[:, h * QK_PAD:(h + 1) * QK_PAD]
        q_rot = pltpu.roll(qh, QK_PAD - MLA_ROPE_DIM, 1)
        q_out[0, h] = (qh * cosq + q_rot * sinq).astype(q_out.dtype)
        k_out[0, h] = kf[:, h * QK_PAD:(h + 1) * QK_PAD].astype(k_out.dtype)


def _mla_prep(z_m, params, tabs):
    B, T, _ = z_m.shape
    tt = min(TILE_MLA_PREP, T)
    full = lambda a: pl.BlockSpec(a.shape, lambda b, i: (0,) * a.ndim)
    tab = lambda a: pl.BlockSpec((tt, a.shape[1]), lambda b, i: (i, 0))
    hspec = pl.BlockSpec((1, MLA_HEADS, tt, QK_PAD), lambda b, i: (b, 0, i, 0))
    vt_rows = D_MLA
    return pl.pallas_call(
        _mla_prep_kernel,
        grid=(B, T // tt),
        in_specs=[pl.BlockSpec((1, tt, MLA_COLS_PAD), lambda b, i: (b, i, 0))]
        + [full(p) for p in params] + [tab(t) for t in tabs],
        out_specs=[hspec, hspec, pl.BlockSpec((1, vt_rows, tt), lambda b, i: (b, 0, i))],
        out_shape=[jax.ShapeDtypeStruct((B, MLA_HEADS, T, QK_PAD), BF16),
                   jax.ShapeDtypeStruct((B, MLA_HEADS, T, QK_PAD), BF16),
                   jax.ShapeDtypeStruct((B, vt_rows, T), BF16)],
        compiler_params=pltpu.CompilerParams(dimension_semantics=("parallel", "parallel"),
                                             vmem_limit_bytes=VMEM_LIMIT),
        name="mla_prep",
    )(z_m, *params, *tabs)


def _attn_kernel(q_ref, k_ref, vt_ref, og_ref, o_ref):
    def scores(h):
        return _dot_nt(k_ref[0, h], q_ref[0, h])

    outs = []
    s_next = scores(0)
    for h in range(MLA_HEADS):
        s = s_next
        if h + 1 < MLA_HEADS:
            s_next = scores(h + 1)
        m = jnp.max(s, axis=0, keepdims=True)
        p = jnp.exp2(s - m)
        l = jnp.sum(p, axis=0, keepdims=True)
        o = jnp.dot(vt_ref[0, h * MLA_V_DIM:(h + 1) * MLA_V_DIM, :], p.astype(BF16),
                    preferred_element_type=F32)
        outs.append(o / l)
    o = jnp.concatenate(outs, axis=0)
    y = o * lax.rsqrt(jnp.mean(o * o, axis=0, keepdims=True) + NORM_EPS) * og_ref[...]
    o_ref[0] = y.T.astype(o_ref.dtype)


def _attn(q, k, vt, out_g_col):
    B, H, T, _ = q.shape
    tq = min(TILE_ATTN_Q, T)
    return pl.pallas_call(
        _attn_kernel,
        grid=(B, T // tq),
        in_specs=[pl.BlockSpec((1, H, tq, QK_PAD), lambda b, i: (b, 0, i, 0)),
                  pl.BlockSpec((1, H, T, QK_PAD), lambda b, i: (b, 0, 0, 0)),
                  pl.BlockSpec((1, vt.shape[1], T), lambda b, i: (b, 0, 0)),
                  pl.BlockSpec(out_g_col.shape, lambda b, i: (0, 0))],
        out_specs=pl.BlockSpec((1, tq, D_MLA), lambda b, i: (b, i, 0)),
        out_shape=jax.ShapeDtypeStruct((B, T, D_MLA), BF16),
        compiler_params=pltpu.CompilerParams(dimension_semantics=("parallel", "arbitrary"),
                                             vmem_limit_bytes=VMEM_LIMIT),
        name="attn",
    )(q, k, vt, out_g_col)


def _mix_out_kernel(yf_ref, yb_ref, bonus_ref, g_ref, ym_ref, x_ref, lng_ref, lnb_ref, ones_ref,
                    wo_r_ref, wo_m_ref, fg_ref, h_out, n_out):
    y = yf_ref[...].astype(F32) + yb_ref[...].astype(F32)
    ones_bd = ones_ref[...]
    inv_n = 1.0 / HEAD_DIM
    mu = _dot(y, ones_bd) * inv_n
    yc = y - mu
    var = _dot(yc * yc, ones_bd) * inv_n
    gn = yc * lax.rsqrt(var + GN_EPS) * lng_ref[...] + lnb_ref[...]
    y_rwkv = (gn + bonus_ref[...].astype(F32)) * g_ref[...].astype(F32)
    h = (x_ref[...] + _dot(y_rwkv, wo_r_ref[...])
         + jnp.dot(ym_ref[...], wo_m_ref[...], preferred_element_type=F32))
    h_out[...] = h
    n_out[...] = _rms(h, fg_ref[...]).astype(n_out.dtype)


def _mix_out(y_f, y_b, bonus, g, y_mla, x2, params):
    n_tok = x2.shape[0]
    tm = min(TILE_MIX_OUT, n_tok)
    full = lambda a: pl.BlockSpec(a.shape, lambda i: (0,) * a.ndim)
    tok = lambda w: pl.BlockSpec((tm, w), lambda i: (i, 0))
    return pl.pallas_call(
        _mix_out_kernel,
        grid=(n_tok // tm,),
        in_specs=[tok(D_RWKV)] * 4 + [tok(D_MLA), tok(D_MODEL)] + [full(p) for p in params],
        out_specs=[tok(D_MODEL), tok(D_MODEL)],
        out_shape=[jax.ShapeDtypeStruct((n_tok, D_MODEL), F32),
                   jax.ShapeDtypeStruct((n_tok, D_MODEL), BF16)],
        compiler_params=pltpu.CompilerParams(dimension_semantics=("parallel",),
                                             vmem_limit_bytes=VMEM_LIMIT),
        name="mix_out",
    )(y_f, y_b, bonus, g, y_mla, x2, *params)


def _ffn_kernel(n_ref, h_ref, wg_ref, wv_ref, cg_ref, cv_ref, bg_ref, bv_ref, wd_ref, fg_ref,
                o_ref, npad_ref, act_ref):
    i = pl.program_id(1)
    T = n_ref.shape[1]
    R = o_ref.shape[1]
    H = FFN_HALO

    @pl.when(i == 0)
    def _():
        zeros = jnp.zeros((H, D_MODEL), npad_ref.dtype)
        npad_ref[0:H, :] = zeros
        npad_ref[H + T:H + T + H, :] = zeros
        npad_ref[H:H + T, :] = n_ref[0]

    n_win = npad_ref[pl.ds(pl.multiple_of(i * R, R), R + 2 * H), :]

    def conv(u, c, b):
        return c[0:1] * u[H - 1:H - 1 + R] + c[1:2] * u[H:H + R] + c[2:3] * u[H + 1:H + 1 + R] + b

    fc = FFN_COL_CHUNK
    n_chunks = D_FF // fc
    f = None
    g0 = 0
    for j in range(n_chunks):
        cs = slice(j * fc, (j + 1) * fc)
        gate = conv(jnp.dot(n_win, wg_ref[:, cs], preferred_element_type=F32), cg_ref[:, cs], bg_ref[:, cs])
        val = conv(jnp.dot(n_win, wv_ref[:, cs], preferred_element_type=F32), cv_ref[:, cs], bv_ref[:, cs])
        act_ref[:, cs] = (gate * _sigmoid(gate) * val).astype(act_ref.dtype)
        if (j + 1) % FFN_DOWN_GROUP == 0 or j + 1 == n_chunks:
            gs = slice(g0 * fc, (j + 1) * fc)
            part = jnp.dot(act_ref[:, gs], wd_ref[gs, :], preferred_element_type=F32)
            f = part if f is None else f + part
            g0 = j + 1
    o_ref[0] = _rms(h_ref[0] + f, fg_ref[...])


def _ffn(n2, h1, w_up, conv_w, conv_b, w_down, final_g):
    B, T, _ = n2.shape
    R = min(TILE_FFN, T)
    once = lambda a: pl.BlockSpec(a.shape, lambda b, i: (0,) * a.ndim, pipeline_mode=pl.Buffered(1))
    half = lambda a, j: pl.BlockSpec((a.shape[0], D_FF), lambda b, i: (0, j), pipeline_mode=pl.Buffered(1))
    rows = pl.BlockSpec((1, R, D_MODEL), lambda b, i: (b, i, 0))
    return pl.pallas_call(
        _ffn_kernel,
        grid=(B, T // R),
        in_specs=[pl.BlockSpec((1, T, D_MODEL), lambda b, i: (b, 0, 0)), rows,
                  half(w_up, 0), half(w_up, 1), half(conv_w, 0), half(conv_w, 1),
                  half(conv_b, 0), half(conv_b, 1), once(w_down), once(final_g)],
        out_specs=rows,
        out_shape=jax.ShapeDtypeStruct((B, T, D_MODEL), F32),
        scratch_shapes=[pltpu.VMEM((T + 2 * FFN_HALO, D_MODEL), BF16),
                        pltpu.VMEM((R, D_FF), BF16)],
        compiler_params=pltpu.CompilerParams(dimension_semantics=("parallel", "arbitrary"),
                                             vmem_limit_bytes=VMEM_LIMIT),
        name="ffn",
    )(n2, h1, w_up, w_up, conv_w, conv_w, conv_b, conv_b, w_down, final_g)


def _rot_cols(w):
    half = w.shape[-1] // 2
    return jnp.concatenate([-w[..., half:], w[..., :half]], axis=-1)


def _block_diag2(a, b):
    za = jnp.zeros((a.shape[0], b.shape[1]), a.dtype)
    zb = jnp.zeros((b.shape[0], a.shape[1]), b.dtype)
    return jnp.concatenate([jnp.concatenate([a, za], axis=1), jnp.concatenate([zb, b], axis=1)], axis=0)


def _shift_diff_matrices(n):
    eye = np.eye(n, dtype=np.float32)
    d_prev = np.eye(n, k=-1, dtype=np.float32) - eye
    d_next = np.eye(n, k=1, dtype=np.float32) - eye
    return jnp.asarray(d_prev, dtype=BF16), jnp.asarray(d_next, dtype=BF16)


def _head_ones():
    idx = np.arange(D_RWKV) // HEAD_DIM
    return jnp.asarray((idx[:, None] == idx[None, :]).astype(np.float32), dtype=BF16)


def _rope_tables(T):
    inv_freq = jnp.power(ROPE_THETA, -jnp.arange(0, MLA_ROPE_DIM, 2, dtype=F32) / MLA_ROPE_DIM)
    ang = jnp.arange(T, dtype=F32)[:, None] * inv_freq[None, :]
    ang = jnp.concatenate([ang, ang], axis=-1)
    cos, sin = jnp.cos(ang), jnp.sin(ang)
    one = jnp.ones((T, MLA_NOPE_DIM), F32)
    zn = jnp.zeros((T, MLA_NOPE_DIM), F32)
    zp = jnp.zeros((T, QK_PAD - MLA_NOPE_DIM - MLA_ROPE_DIM), F32)
    q_scale = MLA_SCALE * math.log2(math.e)
    cosq = jnp.concatenate([one, cos, zp], axis=1) * q_scale
    sinq = jnp.concatenate([zn, sin, zp], axis=1) * q_scale
    cs = jnp.concatenate([cos, sin, jnp.zeros((T, 128 - 2 * MLA_ROPE_DIM), F32)], axis=1)
    return cosq, sinq, cs


def _pad_heads(w, src_width, src_off, n, dst_off):
    K = w.shape[0]
    w = w.reshape(K, MLA_HEADS, src_width)[:, :, src_off:src_off + n]
    out = jnp.zeros((K, MLA_HEADS, QK_PAD), w.dtype)
    out = out.at[:, :, dst_off:dst_off + n].set(w)
    return out.reshape(K, MLA_HEADS * QK_PAD)


def kernel(x, ln_mix_g, w_in, shift_mu_prev, shift_mu_next, decay_w0_fwd, decay_w2_fwd, decay_w0_bwd, decay_w2_bwd, iclr_a0_fwd, iclr_a2_fwd, iclr_a0_bwd, iclr_a2_bwd, gate_g2, k_k, k_a, r_k, ln_x_g, ln_x_b, q_norm_g, w_uq, kv_norm_g, w_ukv, mla_out_g, w_out, ln_ffn_g, w_ffn_up, ffn_conv_w, ffn_conv_b, w_ffn_down, ln_final_g):
    B, T, D = x.shape
    assert w_in.shape[0] == 1, "single-layer block"
    n_tok = B * T
    row = lambda a: a.reshape(1, -1).astype(F32)
    l = 0

    w_in_l = w_in[l]
    w_r = w_in_l[:, :RWKV_COLS].astype(BF16)
    w_kr = w_in_l[:, RWKV_COLS + Q_LORA_RANK + KV_LORA_RANK:]
    w_m = jnp.concatenate([w_in_l[:, RWKV_COLS:], _rot_cols(w_kr),
                           jnp.zeros((D, MLA_COLS_PAD - (Q_LORA_RANK + KV_LORA_RANK + 2 * MLA_ROPE_DIM)), F32)],
                          axis=1).astype(BF16)
    ones_bd = _head_ones()
    prep_params = [
        row(shift_mu_prev[l]), row(shift_mu_next[l]),
        row(jnp.concatenate([decay_w0_fwd[l], decay_w0_bwd[l]])),
        _block_diag2(decay_w2_fwd[l], decay_w2_bwd[l]).astype(BF16),
        row(jnp.concatenate([iclr_a0_fwd[l], iclr_a0_bwd[l]])),
        _block_diag2(iclr_a2_fwd[l], iclr_a2_bwd[l]).astype(BF16),
        gate_g2[l].astype(BF16),
        row(k_k[l]), row(k_a[l]), row(r_k[l]), ones_bd, *_shift_diff_matrices(min(TILE_RWKV_PREP, T)),
    ]
    qw = MLA_NOPE_DIM + MLA_ROPE_DIM
    w_uq_l = w_uq[l]
    w_uq_h = w_uq_l.reshape(Q_LORA_RANK, MLA_HEADS, qw)
    wq_all = jnp.concatenate([w_uq_h, _rot_cols(w_uq_h[:, :, MLA_NOPE_DIM:])], axis=-1)
    wq_all = wq_all.reshape(Q_LORA_RANK, MLA_HEADS * QK_PAD).astype(BF16)
    kvw = MLA_NOPE_DIM + MLA_V_DIM
    w_ukv_l = w_ukv[l]
    wk_pad = _pad_heads(w_ukv_l, kvw, 0, MLA_NOPE_DIM, 0).astype(BF16)
    wv_t = w_ukv_l.reshape(KV_LORA_RANK, MLA_HEADS, kvw)[:, :, MLA_NOPE_DIM:].reshape(KV_LORA_RANK, D_MLA).T.astype(BF16)
    e2 = np.zeros((128, MLA_HEADS * QK_PAD), np.float32)
    for h in range(MLA_HEADS):
        for j in range(MLA_ROPE_DIM):
            e2[j, h * QK_PAD + MLA_NOPE_DIM + j] = 1.0
            e2[MLA_ROPE_DIM + j, h * QK_PAD + MLA_NOPE_DIM + j] = 1.0
    e2 = jnp.asarray(e2, dtype=BF16)
    cosq, sinq, cs = _rope_tables(T)
    mla_params = [row(q_norm_g[l]), row(kv_norm_g[l]), wq_all, wk_pad, wv_t, e2]
    w_out_l = w_out[l]
    mix_params = [row(ln_x_g[l]), row(ln_x_b[l]), ones_bd,
                  w_out_l[:D_RWKV].astype(BF16), w_out_l[D_RWKV:].astype(BF16), row(ln_ffn_g[l])]
    w_up = w_ffn_up[l]
    conv_w = ffn_conv_w[l]
    conv_b = ffn_conv_b[l]

    x2 = x.reshape(n_tok, D)
    z_r, z_m = _inproj(x2, row(ln_mix_g[l]), w_r, w_m)
    z_r = z_r.reshape(B, T, RWKV_COLS)
    z_m = z_m.reshape(B, T, MLA_COLS_PAD)

    r, k, v, kk, a_f, a_b, lw_f, lw_b, bonus, g = _rwkv_prep(z_r, prep_params)
    y_f, y_b = _wkv(r, k, v, kk, a_f, a_b, lw_f, lw_b, row(k_a[l]))

    q, kf, vt_mla = _mla_prep(z_m, mla_params, [cosq, sinq, cs])
    y_mla = _attn(q, kf, vt_mla, mla_out_g[l].reshape(D_MLA, 1).astype(F32))

    flat = lambda a: a.reshape(n_tok, a.shape[-1])
    h1, n2 = _mix_out(flat(y_f), flat(y_b), flat(bonus), flat(g), flat(y_mla), x2, mix_params)

    return _ffn(n2.reshape(B, T, D), h1.reshape(B, T, D),
                w_up.astype(BF16), conv_w, row(conv_b), w_ffn_down[l].astype(BF16), row(ln_final_g))
```

```python
import functools
import math

import numpy as np
import jax
import jax.numpy as jnp
from jax import lax
from jax.experimental import pallas as pl
from jax.experimental.pallas import tpu as pltpu

F32 = jnp.float32
BF16 = jnp.bfloat16

D_MODEL = 1024
RWKV_HEADS = 8
HEAD_DIM = 64
D_RWKV = RWKV_HEADS * HEAD_DIM
DECAY_LORA = 64
ICLR_LORA = 64
GATE_LORA = 128
GN_EPS = 64e-5
L2_EPS = 1e-12
MLA_HEADS = 8
MLA_NOPE_DIM = 64
MLA_ROPE_DIM = 32
MLA_V_DIM = 64
D_MLA = MLA_HEADS * MLA_V_DIM
Q_LORA_RANK = 768
KV_LORA_RANK = 256
ROPE_THETA = 10000.0
MLA_SCALE = (MLA_NOPE_DIM + MLA_ROPE_DIM) ** -0.5
RWKV_COLS = 3 * D_RWKV + 2 * DECAY_LORA + 2 * ICLR_LORA + GATE_LORA
MLA_COLS_PAD = 1152
D_FF = 2816
NORM_EPS = 1e-6
QK_PAD = 128
VMEM_LIMIT = 56 * 1024 * 1024
BF16_SUBLANES = 16

WKV_CHUNK = 64
DECAY_LOG_SCALE = -math.exp(-0.5)

TILE_INPROJ = 1024
TILE_RWKV_PREP = 512
RWKV_SHIFT_BLOCK = 256
TILE_WKV = 512
WKV_HEADS_PER_STEP = 8
WKV_BATCH_PER_STEP = 2
WKV_UNROLL = 8
TILE_MLA_PREP = 1024
TILE_ATTN_Q = 512
TILE_MIX_OUT = 1024
TILE_FFN = 512
FFN_COL_CHUNK = 256
FFN_DOWN_GROUP = 6
FFN_HALO = BF16_SUBLANES


def _dot(a, b):
    return jnp.dot(a.astype(BF16), b.astype(BF16), preferred_element_type=F32)


def _dot_nt(a, b):
    return lax.dot_general(a.astype(BF16), b.astype(BF16), (((1,), (1,)), ((), ())),
                           preferred_element_type=F32)


def _sigmoid(x):
    return 1.0 / (1.0 + jnp.exp(-x))


def _rms(x, g):
    return x * lax.rsqrt(jnp.mean(x * x, axis=-1, keepdims=True) + NORM_EPS) * g


def _inproj_kernel(x_ref, g_ref, wr_ref, wm_ref, zr_ref, zm_ref):
    n = _rms(x_ref[...], g_ref[...]).astype(BF16)
    zr_ref[...] = jnp.dot(n, wr_ref[...], preferred_element_type=F32).astype(zr_ref.dtype)
    zm_ref[...] = jnp.dot(n, wm_ref[...], preferred_element_type=F32).astype(zm_ref.dtype)


def _inproj(x2, g, w_r, w_m):
    n_tok = x2.shape[0]
    tm = min(TILE_INPROJ, n_tok)
    full = lambda a: pl.BlockSpec(a.shape, lambda i: (0,) * a.ndim)
    return pl.pallas_call(
        _inproj_kernel,
        grid=(n_tok // tm,),
        in_specs=[pl.BlockSpec((tm, D_MODEL), lambda i: (i, 0)), full(g), full(w_r), full(w_m)],
        out_specs=[pl.BlockSpec((tm, RWKV_COLS), lambda i: (i, 0)),
                   pl.BlockSpec((tm, MLA_COLS_PAD), lambda i: (i, 0))],
        out_shape=[jax.ShapeDtypeStruct((n_tok, RWKV_COLS), BF16),
                   jax.ShapeDtypeStruct((n_tok, MLA_COLS_PAD), BF16)],
        compiler_params=pltpu.CompilerParams(dimension_semantics=("parallel",),
                                             vmem_limit_bytes=VMEM_LIMIT),
        name="inproj",
    )(x2, g, w_r, w_m)


def _rwkv_prep_kernel(z_ref, zp_ref, zn_ref, mup_ref, mun_ref, w0_ref, w2_ref, a0_ref, a2_ref,
                      g2_ref, kk_ref, ka_ref, rk_ref, ones_ref, dprev_ref, dnext_ref,
                      r_out, k_out, v_out, kkn_out, af_out, ab_out, lwf_out, lwb_out,
                      bonus_out, g_out):
    i = pl.program_id(1)
    nt = pl.num_programs(1)
    z16 = z_ref[0]
    z = z16.astype(F32)
    tt = z.shape[0]
    sb = dprev_ref.shape[0]
    has_prev = (i > 0).astype(F32)
    has_next = (i < nt - 1).astype(F32)
    prev_row = zp_ref[0][BF16_SUBLANES - 1:BF16_SUBLANES, :].astype(F32) * has_prev
    next_row = zn_ref[0][0:1, :].astype(F32) * has_next
    row8 = lax.broadcasted_iota(jnp.int32, (8, 1), 0)
    dps, dns = [], []
    for h in range(tt // sb):
        zb = z16[h * sb:(h + 1) * sb]
        before = prev_row if h == 0 else z[h * sb - 1:h * sb]
        after = next_row if (h + 1) * sb == tt else z[(h + 1) * sb:(h + 1) * sb + 1]
        dp = jnp.dot(dprev_ref[...], zb, preferred_element_type=F32)
        dn = jnp.dot(dnext_ref[...], zb, preferred_element_type=F32)
        dps += [dp[:8] + jnp.where(row8 == 0, before, 0.0), dp[8:]]
        dns += [dn[:sb - 8], dn[sb - 8:] + jnp.where(row8 == 7, after, 0.0)]
    d_prev = jnp.concatenate(dps, axis=0)
    d_next = jnp.concatenate(dns, axis=0)
    zs = z + mup_ref[...] * d_prev + mun_ref[...] * d_next

    d = D_RWKV
    r = zs[:, 0:d]
    k = zs[:, d:2 * d]
    v = zs[:, 2 * d:3 * d]
    wd = zs[:, 3 * d:3 * d + 128]
    ad = zs[:, 3 * d + 128:3 * d + 256]
    gd = zs[:, 3 * d + 256:3 * d + 384]

    logit = w0_ref[...] + _dot(jnp.tanh(wd), w2_ref[...])
    logw = DECAY_LOG_SCALE * _sigmoid(logit)
    a = _sigmoid(a0_ref[...] + _dot(ad, a2_ref[...]))
    g = _dot(_sigmoid(gd), g2_ref[...])
    a_f = a[:, :d]
    a_b = a[:, d:]

    ones_bd = ones_ref[...]
    kk = k * kk_ref[...]
    ss = _dot(kk * kk, ones_bd)
    kk = kk * lax.rsqrt(jnp.maximum(ss, L2_EPS * L2_EPS))
    k_a = ka_ref[...]
    k_fb = k * (2.0 + (a_f + a_b - 2.0) * k_a)
    bonus = _dot(r * k_fb * rk_ref[...], ones_bd) * v

    r_out[0] = r.astype(r_out.dtype)
    k_out[0] = k.astype(k_out.dtype)
    v_out[0] = v.astype(v_out.dtype)
    kkn_out[0] = kk.astype(kkn_out.dtype)
    af_out[0] = a_f.astype(af_out.dtype)
    ab_out[0] = a_b.astype(ab_out.dtype)
    lwf_out[0] = logw[:, :d]
    lwb_out[0] = logw[:, d:]
    bonus_out[0] = bonus.astype(bonus_out.dtype)
    g_out[0] = g.astype(g_out.dtype)


def _rwkv_prep(z_r, params):
    B, T, _ = z_r.shape
    tt = min(TILE_RWKV_PREP, T)
    nt = T // tt
    hr = BF16_SUBLANES
    hb = tt // hr
    full = lambda a: pl.BlockSpec(a.shape, lambda b, i: (0,) * a.ndim)
    seq = lambda w: pl.BlockSpec((1, tt, w), lambda b, i: (b, i, 0))
    in_specs = [
        seq(RWKV_COLS),
        pl.BlockSpec((1, hr, RWKV_COLS), lambda b, i: (b, jnp.maximum(i * hb - 1, 0), 0)),
        pl.BlockSpec((1, hr, RWKV_COLS), lambda b, i: (b, jnp.minimum((i + 1) * hb, T // hr - 1), 0)),
    ] + [full(p) for p in params]
    seq16 = jax.ShapeDtypeStruct((B, T, D_RWKV), BF16)
    seq32 = jax.ShapeDtypeStruct((B, T, D_RWKV), F32)
    return pl.pallas_call(
        _rwkv_prep_kernel,
        grid=(B, nt),
        in_specs=in_specs,
        out_specs=[seq(D_RWKV)] * 10,
        out_shape=[seq16, seq16, seq16, seq16, seq16, seq16, seq32, seq32, seq16, seq16],
        compiler_params=pltpu.CompilerParams(dimension_semantics=("parallel", "parallel"),
                                             vmem_limit_bytes=VMEM_LIMIT),
        name="rwkv_prep",
    )(z_r, z_r, z_r, *params)


def _wkv_operands(bi, rows, reverse, r_ref, k_ref, v_ref, kk_ref, a_ref, lw_ref, ka_ref):
    C = WKV_CHUNK
    lw = lw_ref[bi, rows, :]
    ti = lax.broadcasted_iota(jnp.int32, (C, C), 0)
    si = lax.broadcasted_iota(jnp.int32, (C, C), 1)
    if reverse:
        incl = si >= ti
        strict = si > ti
    else:
        incl = si <= ti
        strict = si < ti
    tri = incl.astype(BF16)
    lw_hi = lw.astype(BF16)
    lw_lo = (lw - lw_hi.astype(F32)).astype(BF16)
    cum = (jnp.dot(tri, lw_hi, preferred_element_type=F32)
           + jnp.dot(tri, lw_lo, preferred_element_type=F32))
    e_inc = jnp.exp(cum)
    e_dec = jnp.exp(-cum)
    e_exc = jnp.exp(cum - lw)
    r = r_ref[bi, rows, :].astype(F32)
    k = k_ref[bi, rows, :].astype(F32)
    kk = kk_ref[bi, rows, :].astype(F32)
    a = a_ref[bi, rows, :].astype(F32)
    k_dir = k * (1.0 + (a - 1.0) * ka_ref[...])
    t2 = lax.broadcasted_iota(jnp.int32, (C, 2 * C), 0)
    l2 = lax.broadcasted_iota(jnp.int32, (C, 2 * C), 1)
    s2 = jnp.bitwise_and(l2, C - 1)
    hi = l2 >= C
    if reverse:
        incl2 = s2 >= t2
        strict2 = s2 > t2
    else:
        incl2 = s2 <= t2
        strict2 = s2 < t2
    return dict(
        incl2=incl2, strict2=strict2, lo=jnp.logical_not(hi), hi=hi,
        sign_lo_neg=jnp.where(hi, 1.0, -1.0).astype(F32), eye2=(s2 == t2).astype(F32),
        p_end=jnp.exp(jnp.sum(lw, axis=0, keepdims=True)),
        alpha=(e_exc * kk).astype(BF16), beta=(e_dec * kk * a).astype(BF16),
        kappa=(e_dec * k_dir).astype(BF16), rho=(e_inc * r).astype(BF16),
        v=v_ref[bi, rows, :])


def _wkv_step(dirs, n_pairs):
    C = WKV_CHUNK
    W = 2 * C
    cat = jnp.concatenate
    zc = jnp.zeros((C, W), BF16)
    lo2 = lax.broadcasted_iota(jnp.int32, (W, W), 1) < C

    def keep(mask, a):
        return jnp.where(mask, a, jnp.zeros_like(a))

    def bd(a, lo, hi):
        return cat([keep(lo, a), keep(hi, a)], axis=0)

    def bd2(a_e, a_o):
        return cat([cat([a_e, zc], axis=1), cat([zc, a_o], axis=1)], axis=0)

    chains = []
    for d in dirs:
        ops = d["ops"]
        for pi in range(n_pairs):
            sl = slice(pi * W, (pi + 1) * W)
            v = ops["v"][:, sl]
            vt = cat([keep(ops["hi"], v), keep(ops["lo"], v)], axis=0).astype(F32).T.astype(BF16)
            chains.append(dict(d=d, pi=pi, sl=sl, al=ops["alpha"][:, sl], be=ops["beta"][:, sl],
                               ka=ops["kappa"][:, sl], rh=ops["rho"][:, sl], v=v,
                               vt_e=vt[:C], vt_o=vt[C:]))
    for c in chains:
        ar = cat([c["al"], c["rh"]], axis=0)
        rhs = cat([keep(lo2, cat([c["be"], c["ka"]], axis=0)),
                   keep(jnp.logical_not(lo2), cat([c["ka"], c["be"]], axis=0))], axis=0)
        g = _dot_nt(ar, rhs)
        c["g_e"], c["g_o"] = g[:, :W], g[:, W:]
    for c in chains:
        ops = c["d"]["ops"]
        lo, hi, strict2, incl2 = ops["lo"], ops["hi"], ops["strict2"], ops["incl2"]
        ga_e, gr_e = c["g_e"][:C], c["g_e"][C:]
        ga_o, gr_o = c["g_o"][:C], c["g_o"][C:]
        c["m"] = jnp.where(strict2, jnp.where(lo, -ga_e, -ga_o), 0.0).astype(BF16)
        ak_e = jnp.where(strict2, ga_e, 0.0).astype(BF16)
        ak_o = jnp.where(strict2, ga_o, 0.0).astype(BF16)
        c["wt_rhs"] = bd2(jnp.where(lo, c["al"], ak_e), jnp.where(hi, c["al"], ak_o))
        rbk_e = jnp.where(incl2, gr_e, 0.0) * ops["sign_lo_neg"]
        rbk_o = jnp.where(incl2, gr_o, 0.0) * (-ops["sign_lo_neg"])
        c["rbk"] = cat([rbk_e, rbk_o], axis=1).astype(BF16)
        c["P"] = ops["eye2"] + c["m"].astype(F32)
    for c in chains:
        ops = c["d"]["ops"]
        c["mp"] = jnp.dot(c["m"], bd(c["m"], ops["lo"], ops["hi"]), preferred_element_type=F32)
    n_sq = int(math.log2(C)) - 1
    for s in range(n_sq):
        last = s == n_sq - 1
        for c in chains:
            ops = c["d"]["ops"]
            mp = c["mp"].astype(BF16)
            p16 = c["P"].astype(BF16)
            out = jnp.dot(p16 if last else cat([p16, mp], axis=0), bd(mp, ops["lo"], ops["hi"]),
                          preferred_element_type=F32)
            c["P"] = c["P"] + out[:C]
            if not last:
                c["mp"] = out[C:]
    for c in chains:
        wt = _dot(c["P"], c["wt_rhs"]).astype(BF16)
        c["wt_e"], c["wt_o"] = wt[:, :W], wt[:, W:]
    for c in chains:
        ops = c["d"]["ops"]
        lo, hi = ops["lo"], ops["hi"]
        c["S"] = c["d"]["s_ref"][c["d"]["bi"], c["pi"]]
        s16 = c["S"].astype(BF16)
        sv_e = keep(lo, s16) + c["vt_e"]
        sv_o = keep(hi, s16) + c["vt_o"]
        lhs = cat([cat([c["wt_e"], keep(lo, c["rh"])], axis=0),
                   cat([c["wt_o"], keep(hi, c["rh"])], axis=0)], axis=1)
        c["x"] = _dot_nt(lhs, bd2(sv_e, sv_o))
        c["ut"] = _dot_nt(cat([sv_e, sv_o], axis=1), bd2(c["wt_e"], c["wt_o"]))
    for c in chains:
        ops = c["d"]["ops"]
        lo, hi = ops["lo"], ops["hi"]
        u = c["x"][:C].astype(BF16)
        vu = cat([keep(lo, u), keep(lo, c["v"]), keep(hi, c["v"]), keep(hi, u)], axis=0)
        c["y"] = c["x"][C:] + jnp.dot(c["rbk"], vu, preferred_element_type=F32)
        ut = c["ut"].astype(BF16)
        lhs = cat([c["vt_e"] - keep(lo, ut), c["vt_o"] - keep(hi, ut)], axis=1)
        kb = cat([keep(lo, c["be"]), keep(lo, c["ka"]), keep(hi, c["ka"]), keep(hi, c["be"])], axis=0)
        c["ds"] = jnp.dot(lhs, kb, preferred_element_type=F32)
    for c in chains:
        c["d"]["s_ref"][c["d"]["bi"], c["pi"]] = (c["S"] + c["ds"]) * c["d"]["ops"]["p_end"][:, c["sl"]]
    for d in dirs:
        d["y_ref"][d["bi"], d["rows"], :] =cat([c["y"] for c in chains if c["d"] is d], axis=1).astype(d["y_ref"].dtype)


def _wkv_kernel(rf_ref, kf_ref, vf_ref, kkf_ref, af_ref, lwf_ref,
                rb_ref, kb_ref, vb_ref, kkb_ref, ab_ref, lwb_ref, ka_ref,
                yf_ref, yb_ref, sf_ref, sb_ref, *, n_batch, n_heads, n_chunks):
    @pl.when(pl.program_id(2) == 0)
    def _():
        sf_ref[...] = jnp.zeros_like(sf_ref)
        sb_ref[...] = jnp.zeros_like(sb_ref)

    C = WKV_CHUNK

    def body(j, carry):
        jb = n_chunks - 1 - j
        rows_f = pl.ds(pl.multiple_of(j * C, C), C)
        rows_b = pl.ds(pl.multiple_of(jb * C, C), C)
        dirs = []
        for bi in range(n_batch):
            dirs.append(dict(bi=bi, rows=rows_f, y_ref=yf_ref, s_ref=sf_ref,
                             ops=_wkv_operands(bi, rows_f, False, rf_ref, kf_ref, vf_ref, kkf_ref,
                                               af_ref, lwf_ref, ka_ref)))
            dirs.append(dict(bi=bi, rows=rows_b, y_ref=yb_ref, s_ref=sb_ref,
                             ops=_wkv_operands(bi, rows_b, True, rb_ref, kb_ref, vb_ref, kkb_ref,
                                               ab_ref, lwb_ref, ka_ref)))
        _wkv_step(dirs, n_heads // 2)
        return carry

    lax.fori_loop(0, n_chunks, body, 0, unroll=min(WKV_UNROLL, n_chunks))


def _wkv(r, k, v, kk, a_f, a_b, lw_f, lw_b, k_a):
    B, T, _ = r.shape
    tt = min(TILE_WKV, T)
    hg = WKV_HEADS_PER_STEP
    nb = WKV_BATCH_PER_STEP if B % WKV_BATCH_PER_STEP == 0 else 1
    nt = T // tt
    ng = RWKV_HEADS // hg
    w = hg * HEAD_DIM
    cpt = tt // WKV_CHUNK
    fwd = pl.BlockSpec((nb, tt, w), lambda b, g, i: (b, i, g))
    bwd = pl.BlockSpec((nb, tt, w), lambda b, g, i: (b, nt - 1 - i, g))
    ka_spec = pl.BlockSpec((1, w), lambda b, g, i: (0, g))
    kern = functools.partial(_wkv_kernel, n_batch=nb, n_heads=hg, n_chunks=cpt)
    return pl.pallas_call(
        kern,
        grid=(B // nb, ng, nt),
        in_specs=[fwd] * 6 + [bwd] * 6 + [ka_spec],
        out_specs=[fwd, bwd],
        out_shape=[jax.ShapeDtypeStruct((B, T, D_RWKV), BF16)] * 2,
        scratch_shapes=[pltpu.VMEM((nb, hg // 2, HEAD_DIM, 2 * HEAD_DIM), F32)] * 2,
        compiler_params=pltpu.CompilerParams(
            dimension_semantics=("parallel", "parallel", "arbitrary"),
            vmem_limit_bytes=VMEM_LIMIT),
        name="wkv",
    )(r, k, v, kk, a_f, lw_f, r, k, v, kk, a_b, lw_b, k_a)


def _mla_prep_kernel(z_ref, qg_ref, kvg_ref, wq_ref, wk_ref, wvt_ref, e2_ref,
                     cosq_ref, sinq_ref, cs_ref, q_out, k_out, vt_out):
    z = z_ref[0].astype(F32)
    c_q = z[:, :Q_LORA_RANK]
    c_kv = z[:, Q_LORA_RANK:Q_LORA_RANK + KV_LORA_RANK]
    kr = z[:, Q_LORA_RANK + KV_LORA_RANK:]
    nq = _rms(c_q, qg_ref[...]).astype(BF16)
    nkv = _rms(c_kv, kvg_ref[...]).astype(BF16)
    q = jnp.dot(nq, wq_ref[...], preferred_element_type=F32)
    kf = (jnp.dot(nkv, wk_ref[...], preferred_element_type=F32)
          + _dot(kr * cs_ref[...], e2_ref[...]))
    vt = lax.dot_general(wvt_ref[...], nkv, (((1,), (1,)), ((), ())), preferred_element_type=F32)
    cosq = cosq_ref[...]
    sinq = sinq_ref[...]
    vt_out[0] = vt.astype(vt_out.dtype)
    for h in range(MLA_HEADS):
        qh = q[:, h * QK_PAD:(h + 1) * QK_PAD]
        q_rot = pltpu.roll(qh, QK_PAD - MLA_ROPE_DIM, 1)
        q_out[0, h] = (qh * cosq + q_rot * sinq).astype(q_out.dtype)
        k_out[0, h] = kf[:, h * QK_PAD:(h + 1) * QK_PAD].astype(k_out.dtype)


def _mla_prep(z_m, params, tabs):
    B, T, _ = z_m.shape
    tt = min(TILE_MLA_PREP, T)
    full = lambda a: pl.BlockSpec(a.shape, lambda b, i: (0,) * a.ndim)
    tab = lambda a: pl.BlockSpec((tt, a.shape[1]), lambda b, i: (i, 0))
    hspec = pl.BlockSpec((1, MLA_HEADS, tt, QK_PAD), lambda b, i: (b, 0, i, 0))
    vt_rows = D_MLA
    return pl.pallas_call(
        _mla_prep_kernel,
        grid=(B, T // tt),
        in_specs=[pl.BlockSpec((1, tt, MLA_COLS_PAD), lambda b, i: (b, i, 0))]
        + [full(p) for p in params] + [tab(t) for t in tabs],
        out_specs=[hspec, hspec, pl.BlockSpec((1, vt_rows, tt), lambda b, i: (b, 0, i))],
        out_shape=[jax.ShapeDtypeStruct((B, MLA_HEADS, T, QK_PAD), BF16),
                   jax.ShapeDtypeStruct((B, MLA_HEADS, T, QK_PAD), BF16),
                   jax.ShapeDtypeStruct((B, vt_rows, T), BF16)],
        compiler_params=pltpu.CompilerParams(dimension_semantics=("parallel", "parallel"),
                                             vmem_limit_bytes=VMEM_LIMIT),
        name="mla_prep",
    )(z_m, *params, *tabs)


def _attn_kernel(q_ref, k_ref, vt_ref, og_ref, o_ref):
    def scores(h):
        return _dot_nt(k_ref[0, h], q_ref[0, h])

    outs = []
    s_next = scores(0)
    for h in range(MLA_HEADS):
        s = s_next
        if h + 1 < MLA_HEADS:
            s_next = scores(h + 1)
        m = jnp.max(s, axis=0, keepdims=True)
        p = jnp.exp2(s - m)
        l = jnp.sum(p, axis=0, keepdims=True)
        o = jnp.dot(vt_ref[0, h * MLA_V_DIM:(h + 1) * MLA_V_DIM, :], p.astype(BF16),
                    preferred_element_type=F32)
        outs.append(o / l)
    o = jnp.concatenate(outs, axis=0)
    y = o * lax.rsqrt(jnp.mean(o * o, axis=0, keepdims=True) + NORM_EPS) * og_ref[...]
    o_ref[0] = y.T.astype(o_ref.dtype)


def _attn(q, k, vt, out_g_col):
    B, H, T, _ = q.shape
    tq = min(TILE_ATTN_Q, T)
    return pl.pallas_call(
        _attn_kernel,
        grid=(B, T // tq),
        in_specs=[pl.BlockSpec((1, H, tq, QK_PAD), lambda b, i: (b, 0, i, 0)),
                  pl.BlockSpec((1, H, T, QK_PAD), lambda b, i: (b, 0, 0, 0)),
                  pl.BlockSpec((1, vt.shape[1], T), lambda b, i: (b, 0, 0)),
                  pl.BlockSpec(out_g_col.shape, lambda b, i: (0, 0))],
        out_specs=pl.BlockSpec((1, tq, D_MLA), lambda b, i: (b, i, 0)),
        out_shape=jax.ShapeDtypeStruct((B, T, D_MLA), BF16),
        compiler_params=pltpu.CompilerParams(dimension_semantics=("parallel", "arbitrary"),
                                             vmem_limit_bytes=VMEM_LIMIT),
        name="attn",
    )(q, k, vt, out_g_col)


def _mix_out_kernel(yf_ref, yb_ref, bonus_ref, g_ref, ym_ref, x_ref, lng_ref, lnb_ref, ones_ref,
                    wo_r_ref, wo_m_ref, fg_ref, h_out, n_out):
    y = yf_ref[...].astype(F32) + yb_ref[...].astype(F32)
    ones_bd = ones_ref[...]
    inv_n = 1.0 / HEAD_DIM
    mu = _dot(y, ones_bd) * inv_n
    yc = y - mu
    var = _dot(yc * yc, ones_bd) * inv_n
    gn = yc * lax.rsqrt(var + GN_EPS) * lng_ref[...] + lnb_ref[...]
    y_rwkv = (gn + bonus_ref[...].astype(F32)) * g_ref[...].astype(F32)
    h = (x_ref[...] + _dot(y_rwkv, wo_r_ref[...])
         + jnp.dot(ym_ref[...], wo_m_ref[...], preferred_element_type=F32))
    h_out[...] = h
    n_out[...] = _rms(h, fg_ref[...]).astype(n_out.dtype)


def _mix_out(y_f, y_b, bonus, g, y_mla, x2, params):
    n_tok = x2.shape[0]
    tm = min(TILE_MIX_OUT, n_tok)
    full = lambda a: pl.BlockSpec(a.shape, lambda i: (0,) * a.ndim)
    tok = lambda w: pl.BlockSpec((tm, w), lambda i: (i, 0))
    return pl.pallas_call(
        _mix_out_kernel,
        grid=(n_tok // tm,),
        in_specs=[tok(D_RWKV)] * 4 + [tok(D_MLA), tok(D_MODEL)] + [full(p) for p in params],
        out_specs=[tok(D_MODEL), tok(D_MODEL)],
        out_shape=[jax.ShapeDtypeStruct((n_tok, D_MODEL), F32),
                   jax.ShapeDtypeStruct((n_tok, D_MODEL), BF16)],
        compiler_params=pltpu.CompilerParams(dimension_semantics=("parallel",),
                                             vmem_limit_bytes=VMEM_LIMIT),
        name="mix_out",
    )(y_f, y_b, bonus, g, y_mla, x2, *params)


def _ffn_kernel(n_ref, h_ref, wg_ref, wv_ref, cg_ref, cv_ref, bg_ref, bv_ref, wd_ref, fg_ref,
                o_ref, npad_ref, act_ref):
    i = pl.program_id(1)
    T = n_ref.shape[1]
    R = o_ref.shape[1]
    H = FFN_HALO

    @pl.when(i == 0)
    def _():
        zeros = jnp.zeros((H, D_MODEL), npad_ref.dtype)
        npad_ref[0:H, :] = zeros
        npad_ref[H + T:H + T + H, :] = zeros
        npad_ref[H:H + T, :] = n_ref[0]

    n_win = npad_ref[pl.ds(pl.multiple_of(i * R, R), R + 2 * H), :]

    def conv(u, c, b):
        return c[0:1] * u[H - 1:H - 1 + R] + c[1:2] * u[H:H + R] + c[2:3] * u[H + 1:H + 1 + R] + b

    fc = FFN_COL_CHUNK
    n_chunks = D_FF // fc
    f = None
    g0 = 0
    for j in range(n_chunks):
        cs = slice(j * fc, (j + 1) * fc)
        gate = conv(jnp.dot(n_win, wg_ref[:, cs], preferred_element_type=F32), cg_ref[:, cs], bg_ref[:, cs])
        val = conv(jnp.dot(n_win, wv_ref[:, cs], preferred_element_type=F32), cv_ref[:, cs], bv_ref[:, cs])
        act_ref[:, cs] = (gate * _sigmoid(gate) * val).astype(act_ref.dtype)
        if (j + 1) % FFN_DOWN_GROUP == 0 or j + 1 == n_chunks:
            gs = slice(g0 * fc, (j + 1) * fc)
            part = jnp.dot(act_ref[:, gs], wd_ref[gs, :], preferred_element_type=F32)
            f = part if f is None else f + part
            g0 = j + 1
    o_ref[0] = _rms(h_ref[0] + f, fg_ref[...])


def _ffn(n2, h1, w_up, conv_w, conv_b, w_down, final_g):
    B, T, _ = n2.shape
    R = min(TILE_FFN, T)
    once = lambda a: pl.BlockSpec(a.shape, lambda b, i: (0,) * a.ndim, pipeline_mode=pl.Buffered(1))
    half = lambda a, j: pl.BlockSpec((a.shape[0], D_FF), lambda b, i: (0, j), pipeline_mode=pl.Buffered(1))
    rows = pl.BlockSpec((1, R, D_MODEL), lambda b, i: (b, i, 0))
    return pl.pallas_call(
        _ffn_kernel,
        grid=(B, T // R),
        in_specs=[pl.BlockSpec((1, T, D_MODEL), lambda b, i: (b, 0, 0)), rows,
                  half(w_up, 0), half(w_up, 1), half(conv_w, 0), half(conv_w, 1),
                  half(conv_b, 0), half(conv_b, 1), once(w_down), once(final_g)],
        out_specs=rows,
        out_shape=jax.ShapeDtypeStruct((B, T, D_MODEL), F32),
        scratch_shapes=[pltpu.VMEM((T + 2 * FFN_HALO, D_MODEL), BF16),
                        pltpu.VMEM((R, D_FF), BF16)],
        compiler_params=pltpu.CompilerParams(dimension_semantics=("parallel", "arbitrary"),
                                             vmem_limit_bytes=VMEM_LIMIT),
        name="ffn",
    )(n2, h1, w_up, w_up, conv_w, conv_w, conv_b, conv_b, w_down, final_g)


def _rot_cols(w):
    half = w.shape[-1] // 2
    return jnp.concatenate([-w[..., half:], w[..., :half]], axis=-1)


def _block_diag2(a, b):
    za = jnp.zeros((a.shape[0], b.shape[1]), a.dtype)
    zb = jnp.zeros((b.shape[0], a.shape[1]), b.dtype)
    return jnp.concatenate([jnp.concatenate([a, za], axis=1), jnp.concatenate([zb, b], axis=1)], axis=0)


def _shift_diff_matrices(n):
    eye = np.eye(n, dtype=np.float32)
    d_prev = np.eye(n, k=-1, dtype=np.float32) - eye
    d_next = np.eye(n, k=1, dtype=np.float32) - eye
    return jnp.asarray(d_prev, dtype=BF16), jnp.asarray(d_next, dtype=BF16)


def _head_ones():
    idx = np.arange(D_RWKV) // HEAD_DIM
    return jnp.asarray((idx[:, None] == idx[None, :]).astype(np.float32), dtype=BF16)


def _rope_tables(T):
    inv_freq = jnp.power(ROPE_THETA, -jnp.arange(0, MLA_ROPE_DIM, 2, dtype=F32) / MLA_ROPE_DIM)
    ang = jnp.arange(T, dtype=F32)[:, None] * inv_freq[None, :]
    ang = jnp.concatenate([ang, ang], axis=-1)
    cos, sin = jnp.cos(ang), jnp.sin(ang)
    one = jnp.ones((T, MLA_NOPE_DIM), F32)
    zn = jnp.zeros((T, MLA_NOPE_DIM), F32)
    zp = jnp.zeros((T, QK_PAD - MLA_NOPE_DIM - MLA_ROPE_DIM), F32)
    q_scale = MLA_SCALE * math.log2(math.e)
    cosq = jnp.concatenate([one, cos, zp], axis=1) * q_scale
    sinq = jnp.concatenate([zn, sin, zp], axis=1) * q_scale
    cs = jnp.concatenate([cos, sin, jnp.zeros((T, 128 - 2 * MLA_ROPE_DIM), F32)], axis=1)
    return cosq, sinq, cs


def _pad_heads(w, src_width, src_off, n, dst_off):
    K = w.shape[0]
    w = w.reshape(K, MLA_HEADS, src_width)[:, :, src_off:src_off + n]
    out = jnp.zeros((K, MLA_HEADS, QK_PAD), w.dtype)
    out = out.at[:, :, dst_off:dst_off + n].set(w)
    return out.reshape(K, MLA_HEADS * QK_PAD)


def kernel(x, ln_mix_g, w_in, shift_mu_prev, shift_mu_next, decay_w0_fwd, decay_w2_fwd, decay_w0_bwd, decay_w2_bwd, iclr_a0_fwd, iclr_a2_fwd, iclr_a0_bwd, iclr_a2_bwd, gate_g2, k_k, k_a, r_k, ln_x_g, ln_x_b, q_norm_g, w_uq, kv_norm_g, w_ukv, mla_out_g, w_out, ln_ffn_g, w_ffn_up, ffn_conv_w, ffn_conv_b, w_ffn_down, ln_final_g):
    B, T, D = x.shape
    assert w_in.shape[0] == 1, "single-layer block"
    n_tok = B * T
    row = lambda a: a.reshape(1, -1).astype(F32)
    l = 0

    w_in_l = w_in[l]
    w_r = w_in_l[:, :RWKV_COLS].astype(BF16)
    w_kr = w_in_l[:, RWKV_COLS + Q_LORA_RANK + KV_LORA_RANK:]
    w_m = jnp.concatenate([w_in_l[:, RWKV_COLS:], _rot_cols(w_kr),
                           jnp.zeros((D, MLA_COLS_PAD - (Q_LORA_RANK + KV_LORA_RANK + 2 * MLA_ROPE_DIM)), F32)],
                          axis=1).astype(BF16)
    ones_bd = _head_ones()
    prep_params = [
        row(shift_mu_prev[l]), row(shift_mu_next[l]),
        row(jnp.concatenate([decay_w0_fwd[l], decay_w0_bwd[l]])),
        _block_diag2(decay_w2_fwd[l], decay_w2_bwd[l]).astype(BF16),
        row(jnp.concatenate([iclr_a0_fwd[l], iclr_a0_bwd[l]])),
        _block_diag2(iclr_a2_fwd[l], iclr_a2_bwd[l]).astype(BF16),
        gate_g2[l].astype(BF16),
        row(k_k[l]), row(k_a[l]), row(r_k[l]), ones_bd, *_shift_diff_matrices(min(RWKV_SHIFT_BLOCK, T)),
    ]
    qw = MLA_NOPE_DIM + MLA_ROPE_DIM
    w_uq_l = w_uq[l]
    w_uq_h = w_uq_l.reshape(Q_LORA_RANK, MLA_HEADS, qw)
    wq_all = jnp.concatenate([w_uq_h, _rot_cols(w_uq_h[:, :, MLA_NOPE_DIM:])], axis=-1)
    wq_all = wq_all.reshape(Q_LORA_RANK, MLA_HEADS * QK_PAD).astype(BF16)
    kvw = MLA_NOPE_DIM + MLA_V_DIM
    w_ukv_l = w_ukv[l]
    wk_pad = _pad_heads(w_ukv_l, kvw, 0, MLA_NOPE_DIM, 0).astype(BF16)
    wv_t = w_ukv_l.reshape(KV_LORA_RANK, MLA_HEADS, kvw)[:, :, MLA_NOPE_DIM:].reshape(KV_LORA_RANK, D_MLA).T.astype(BF16)
    e2 = np.zeros((128, MLA_HEADS * QK_PAD), np.float32)
    for h in range(MLA_HEADS):
        for j in range(MLA_ROPE_DIM):
            e2[j, h * QK_PAD + MLA_NOPE_DIM + j] = 1.0
            e2[MLA_ROPE_DIM + j, h * QK_PAD + MLA_NOPE_DIM + j] = 1.0
    e2 = jnp.asarray(e2, dtype=BF16)
    cosq, sinq, cs = _rope_tables(T)
    mla_params = [row(q_norm_g[l]), row(kv_norm_g[l]), wq_all, wk_pad, wv_t, e2]
    w_out_l = w_out[l]
    mix_params = [row(ln_x_g[l]), row(ln_x_b[l]), ones_bd,
                  w_out_l[:D_RWKV].astype(BF16), w_out_l[D_RWKV:].astype(BF16), row(ln_ffn_g[l])]
    w_up = w_ffn_up[l]
    conv_w = ffn_conv_w[l]
    conv_b = ffn_conv_b[l]

    x2 = x.reshape(n_tok, D)
    z_r, z_m = _inproj(x2, row(ln_mix_g[l]), w_r, w_m)
    z_r = z_r.reshape(B, T, RWKV_COLS)
    z_m = z_m.reshape(B, T, MLA_COLS_PAD)

    r, k, v, kk, a_f, a_b, lw_f, lw_b, bonus, g = _rwkv_prep(z_r, prep_params)
    y_f, y_b = _wkv(r, k, v, kk, a_f, a_b, lw_f, lw_b, row(k_a[l]))

    q, kf, vt_mla = _mla_prep(z_m, mla_params, [cosq, sinq, cs])
    y_mla = _attn(q, kf, vt_mla, mla_out_g[l].reshape(D_MLA, 1).astype(F32))

    flat = lambda a: a.reshape(n_tok, a.shape[-1])
    h1, n2 = _mix_out(flat(y_f), flat(y_b), flat(bonus), flat(g), flat(y_mla), x2, mix_params)

    return _ffn(n2.reshape(B, T, D), h1.reshape(B, T, D),
                w_up.astype(BF16), conv_w, row(conv_b), w_ffn_down[l].astype(BF16), row(ln_final_g))
```

```python
import functools
import math

import numpy as np
import jax
import jax.numpy as jnp
from jax import lax
from jax.experimental import pallas as pl
from jax.experimental.pallas import tpu as pltpu

F32 = jnp.float32
BF16 = jnp.bfloat16

D_MODEL = 1024
RWKV_HEADS = 8
HEAD_DIM = 64
D_RWKV = RWKV_HEADS * HEAD_DIM
DECAY_LORA = 64
ICLR_LORA = 64
GATE_LORA = 128
GN_EPS = 64e-5
L2_EPS = 1e-12
MLA_HEADS = 8
MLA_NOPE_DIM = 64
MLA_ROPE_DIM = 32
MLA_V_DIM = 64
D_MLA = MLA_HEADS * MLA_V_DIM
Q_LORA_RANK = 768
KV_LORA_RANK = 256
ROPE_THETA = 10000.0
MLA_SCALE = (MLA_NOPE_DIM + MLA_ROPE_DIM) ** -0.5
RWKV_COLS = 3 * D_RWKV + 2 * DECAY_LORA + 2 * ICLR_LORA + GATE_LORA
MLA_COLS_PAD = 1152
D_FF = 2816
NORM_EPS = 1e-6
QK_PAD = 128
VMEM_LIMIT = 56 * 1024 * 1024
BF16_SUBLANES = 16

WKV_CHUNK = 64
DECAY_LOG_SCALE = -math.exp(-0.5)

TILE_INPROJ = 1024
TILE_RWKV_PREP = 1024
RWKV_SHIFT_BLOCK = 256
TILE_WKV = 512
WKV_HEADS_PER_STEP = 8
WKV_BATCH_PER_STEP = 2
WKV_UNROLL = 8
TILE_MLA_PREP = 1024
TILE_ATTN_Q = 512
TILE_FFN = 512
FFN_COL_CHUNK = 256
FFN_DOWN_GROUP = 6
FFN_HALO = BF16_SUBLANES


def _dot(a, b):
    return jnp.dot(a.astype(BF16), b.astype(BF16), preferred_element_type=F32)


def _dot_nt(a, b):
    return lax.dot_general(a.astype(BF16), b.astype(BF16), (((1,), (1,)), ((), ())),
                           preferred_element_type=F32)


def _sigmoid(x):
    return 1.0 / (1.0 + jnp.exp(-x))


def _rms(x, g):
    return x * lax.rsqrt(jnp.mean(x * x, axis=-1, keepdims=True) + NORM_EPS) * g


def _inproj_kernel(x_ref, g_ref, wr_ref, wm_ref, zr_ref, zm_ref):
    n = _rms(x_ref[...], g_ref[...]).astype(BF16)
    zr_ref[...] = jnp.dot(n, wr_ref[...], preferred_element_type=F32).astype(zr_ref.dtype)
    zm_ref[...] = jnp.dot(n, wm_ref[...], preferred_element_type=F32).astype(zm_ref.dtype)


def _inproj(x2, g, w_r, w_m):
    n_tok = x2.shape[0]
    tm = min(TILE_INPROJ, n_tok)
    full = lambda a: pl.BlockSpec(a.shape, lambda i: (0,) * a.ndim)
    return pl.pallas_call(
        _inproj_kernel,
        grid=(n_tok // tm,),
        in_specs=[pl.BlockSpec((tm, D_MODEL), lambda i: (i, 0)), full(g), full(w_r), full(w_m)],
        out_specs=[pl.BlockSpec((tm, RWKV_COLS), lambda i: (i, 0)),
                   pl.BlockSpec((tm, MLA_COLS_PAD), lambda i: (i, 0))],
        out_shape=[jax.ShapeDtypeStruct((n_tok, RWKV_COLS), BF16),
                   jax.ShapeDtypeStruct((n_tok, MLA_COLS_PAD), BF16)],
        compiler_params=pltpu.CompilerParams(dimension_semantics=("parallel",),
                                             vmem_limit_bytes=VMEM_LIMIT),
        name="inproj",
    )(x2, g, w_r, w_m)


def _rwkv_prep_kernel(z_ref, zp_ref, zn_ref, mup_ref, mun_ref, w0_ref, w2_ref, a0_ref, a2_ref,
                      g2_ref, kk_ref, ka_ref, rk_ref, ones_ref, dprev_ref, dnext_ref,
                      r_out, k_out, v_out, kkn_out, af_out, ab_out, lwf_out, lwb_out,
                      bonus_out, g_out):
    i = pl.program_id(1)
    nt = pl.num_programs(1)
    z16 = z_ref[0]
    z = z16.astype(F32)
    tt = z.shape[0]
    sb = dprev_ref.shape[0]
    has_prev = (i > 0).astype(F32)
    has_next = (i < nt - 1).astype(F32)
    prev_row = zp_ref[0][BF16_SUBLANES - 1:BF16_SUBLANES, :].astype(F32) * has_prev
    next_row = zn_ref[0][0:1, :].astype(F32) * has_next
    row8 = lax.broadcasted_iota(jnp.int32, (8, 1), 0)
    dps, dns = [], []
    for h in range(tt // sb):
        zb = z16[h * sb:(h + 1) * sb]
        before = prev_row if h == 0 else z[h * sb - 1:h * sb]
        after = next_row if (h + 1) * sb == tt else z[(h + 1) * sb:(h + 1) * sb + 1]
        dp = jnp.dot(dprev_ref[...], zb, preferred_element_type=F32)
        dn = jnp.dot(dnext_ref[...], zb, preferred_element_type=F32)
        dps += [dp[:8] + jnp.where(row8 == 0, before, 0.0), dp[8:]]
        dns += [dn[:sb - 8], dn[sb - 8:] + jnp.where(row8 == 7, after, 0.0)]
    d_prev = jnp.concatenate(dps, axis=0)
    d_next = jnp.concatenate(dns, axis=0)
    zs = z + mup_ref[...] * d_prev + mun_ref[...] * d_next

    d = D_RWKV
    r = zs[:, 0:d]
    k = zs[:, d:2 * d]
    v = zs[:, 2 * d:3 * d]
    wd = zs[:, 3 * d:3 * d + 128]
    ad = zs[:, 3 * d + 128:3 * d + 256]
    gd = zs[:, 3 * d + 256:3 * d + 384]

    logit = w0_ref[...] + _dot(jnp.tanh(wd), w2_ref[...])
    logw = DECAY_LOG_SCALE * _sigmoid(logit)
    a = _sigmoid(a0_ref[...] + _dot(ad, a2_ref[...]))
    g = _dot(_sigmoid(gd), g2_ref[...])
    a_f = a[:, :d]
    a_b = a[:, d:]

    ones_bd = ones_ref[...]
    kk = k * kk_ref[...]
    ss = _dot(kk * kk, ones_bd)
    kk = kk * lax.rsqrt(jnp.maximum(ss, L2_EPS * L2_EPS))
    k_a = ka_ref[...]
    k_fb = k * (2.0 + (a_f + a_b - 2.0) * k_a)
    bonus = _dot(r * k_fb * rk_ref[...], ones_bd) * v

    r_out[0] = r.astype(r_out.dtype)
    k_out[0] = k.astype(k_out.dtype)
    v_out[0] = v.astype(v_out.dtype)
    kkn_out[0] = kk.astype(kkn_out.dtype)
    af_out[0] = a_f.astype(af_out.dtype)
    ab_out[0] = a_b.astype(ab_out.dtype)
    lwf_out[0] = logw[:, :d]
    lwb_out[0] = logw[:, d:]
    bonus_out[0] = bonus.astype(bonus_out.dtype)
    g_out[0] = g.astype(g_out.dtype)


def _rwkv_prep(z_r, params):
    B, T, _ = z_r.shape
    tt = min(TILE_RWKV_PREP, T)
    nt = T // tt
    hr = BF16_SUBLANES
    hb = tt // hr
    full = lambda a: pl.BlockSpec(a.shape, lambda b, i: (0,) * a.ndim)
    seq = lambda w: pl.BlockSpec((1, tt, w), lambda b, i: (b, i, 0))
    in_specs = [
        seq(RWKV_COLS),
        pl.BlockSpec((1, hr, RWKV_COLS), lambda b, i: (b, jnp.maximum(i * hb - 1, 0), 0)),
        pl.BlockSpec((1, hr, RWKV_COLS), lambda b, i: (b, jnp.minimum((i + 1) * hb, T // hr - 1), 0)),
    ] + [full(p) for p in params]
    seq16 = jax.ShapeDtypeStruct((B, T, D_RWKV), BF16)
    seq32 = jax.ShapeDtypeStruct((B, T, D_RWKV), F32)
    return pl.pallas_call(
        _rwkv_prep_kernel,
        grid=(B, nt),
        in_specs=in_specs,
        out_specs=[seq(D_RWKV)] * 10,
        out_shape=[seq16, seq16, seq16, seq16, seq16, seq16, seq32, seq32, seq16, seq16],
        compiler_params=pltpu.CompilerParams(dimension_semantics=("parallel", "parallel"),
                                             vmem_limit_bytes=VMEM_LIMIT),
        name="rwkv_prep",
    )(z_r, z_r, z_r, *params)


def _wkv_operands(bi, rows, reverse, r_ref, k_ref, v_ref, kk_ref, a_ref, lw_ref, ka_ref):
    C = WKV_CHUNK
    lw = lw_ref[bi, rows, :]
    ti = lax.broadcasted_iota(jnp.int32, (C, C), 0)
    si = lax.broadcasted_iota(jnp.int32, (C, C), 1)
    if reverse:
        incl = si >= ti
        strict = si > ti
    else:
        incl = si <= ti
        strict = si < ti
    tri = incl.astype(BF16)
    lw_hi = lw.astype(BF16)
    lw_lo = (lw - lw_hi.astype(F32)).astype(BF16)
    cum = (jnp.dot(tri, lw_hi, preferred_element_type=F32)
           + jnp.dot(tri, lw_lo, preferred_element_type=F32))
    e_inc = jnp.exp(cum)
    e_dec = jnp.exp(-cum)
    e_exc = jnp.exp(cum - lw)
    r = r_ref[bi, rows, :].astype(F32)
    k = k_ref[bi, rows, :].astype(F32)
    kk = kk_ref[bi, rows, :].astype(F32)
    a = a_ref[bi, rows, :].astype(F32)
    k_dir = k * (1.0 + (a - 1.0) * ka_ref[...])
    t2 = lax.broadcasted_iota(jnp.int32, (C, 2 * C), 0)
    l2 = lax.broadcasted_iota(jnp.int32, (C, 2 * C), 1)
    s2 = jnp.bitwise_and(l2, C - 1)
    hi = l2 >= C
    if reverse:
        incl2 = s2 >= t2
        strict2 = s2 > t2
    else:
        incl2 = s2 <= t2
        strict2 = s2 < t2
    return dict(
        incl2=incl2, strict2=strict2, lo=jnp.logical_not(hi), hi=hi,
        sign_lo_neg=jnp.where(hi, 1.0, -1.0).astype(F32), eye2=(s2 == t2).astype(F32),
        p_end=jnp.exp(jnp.sum(lw, axis=0, keepdims=True)),
        alpha=(e_exc * kk).astype(BF16), beta=(e_dec * kk * a).astype(BF16),
        kappa=(e_dec * k_dir).astype(BF16), rho=(e_inc * r).astype(BF16),
        v=v_ref[bi, rows, :])


def _wkv_step(dirs, n_pairs):
    C = WKV_CHUNK
    W = 2 * C
    cat = jnp.concatenate
    zc = jnp.zeros((C, W), BF16)
    lo2 = lax.broadcasted_iota(jnp.int32, (W, W), 1) < C

    def keep(mask, a):
        return jnp.where(mask, a, jnp.zeros_like(a))

    def bd(a, lo, hi):
        return cat([keep(lo, a), keep(hi, a)], axis=0)

    def bd2(a_e, a_o):
        return cat([cat([a_e, zc], axis=1), cat([zc, a_o], axis=1)], axis=0)

    chains = []
    for d in dirs:
        ops = d["ops"]
        for pi in range(n_pairs):
            sl = slice(pi * W, (pi + 1) * W)
            v = ops["v"][:, sl]
            vt = cat([keep(ops["hi"], v), keep(ops["lo"], v)], axis=0).astype(F32).T.astype(BF16)
            chains.append(dict(d=d, pi=pi, sl=sl, al=ops["alpha"][:, sl], be=ops["beta"][:, sl],
                               ka=ops["kappa"][:, sl], rh=ops["rho"][:, sl], v=v,
                               vt_e=vt[:C], vt_o=vt[C:]))
    for c in chains:
        ar = cat([c["al"], c["rh"]], axis=0)
        rhs = cat([keep(lo2, cat([c["be"], c["ka"]], axis=0)),
                   keep(jnp.logical_not(lo2), cat([c["ka"], c["be"]], axis=0))], axis=0)
        g = _dot_nt(ar, rhs)
        c["g_e"], c["g_o"] = g[:, :W], g[:, W:]
    for c in chains:
        ops = c["d"]["ops"]
        lo, hi, strict2, incl2 = ops["lo"], ops["hi"], ops["strict2"], ops["incl2"]
        ga_e, gr_e = c["g_e"][:C], c["g_e"][C:]
        ga_o, gr_o = c["g_o"][:C], c["g_o"][C:]
        c["m"] = jnp.where(strict2, jnp.where(lo, -ga_e, -ga_o), 0.0).astype(BF16)
        ak_e = jnp.where(strict2, ga_e, 0.0).astype(BF16)
        ak_o = jnp.where(strict2, ga_o, 0.0).astype(BF16)
        c["wt_rhs"] = bd2(jnp.where(lo, c["al"], ak_e), jnp.where(hi, c["al"], ak_o))
        rbk_e = jnp.where(incl2, gr_e, 0.0) * ops["sign_lo_neg"]
        rbk_o = jnp.where(incl2, gr_o, 0.0) * (-ops["sign_lo_neg"])
        c["rbk"] = cat([rbk_e, rbk_o], axis=1).astype(BF16)
        c["P"] = ops["eye2"] + c["m"].astype(F32)
    for c in chains:
        ops = c["d"]["ops"]
        c["mp"] = jnp.dot(c["m"], bd(c["m"], ops["lo"], ops["hi"]), preferred_element_type=F32)
    n_sq = int(math.log2(C)) - 1
    for s in range(n_sq):
        last = s == n_sq - 1
        for c in chains:
            ops = c["d"]["ops"]
            mp = c["mp"].astype(BF16)
            p16 = c["P"].astype(BF16)
            out = jnp.dot(p16 if last else cat([p16, mp], axis=0), bd(mp, ops["lo"], ops["hi"]),
                          preferred_element_type=F32)
            c["P"] = c["P"] + out[:C]
            if not last:
                c["mp"] = out[C:]
    for c in chains:
        wt = _dot(c["P"], c["wt_rhs"]).astype(BF16)
        c["wt_e"], c["wt_o"] = wt[:, :W], wt[:, W:]
    for c in chains:
        ops = c["d"]["ops"]
        lo, hi = ops["lo"], ops["hi"]
        c["S"] = c["d"]["s_ref"][c["d"]["bi"], c["pi"]]
        s16 = c["S"].astype(BF16)
        sv_e = keep(lo, s16) + c["vt_e"]
        sv_o = keep(hi, s16) + c["vt_o"]
        lhs = cat([cat([c["wt_e"], keep(lo, c["rh"])], axis=0),
                   cat([c["wt_o"], keep(hi, c["rh"])], axis=0)], axis=1)
        c["x"] = _dot_nt(lhs, bd2(sv_e, sv_o))
        c["ut"] = _dot_nt(cat([sv_e, sv_o], axis=1), bd2(c["wt_e"], c["wt_o"]))
    for c in chains:
        ops = c["d"]["ops"]
        lo, hi = ops["lo"], ops["hi"]
        u = c["x"][:C].astype(BF16)
        vu = cat([keep(lo, u), keep(lo, c["v"]), keep(hi, c["v"]), keep(hi, u)], axis=0)
        c["y"] = c["x"][C:] + jnp.dot(c["rbk"], vu, preferred_element_type=F32)
        ut = c["ut"].astype(BF16)
        lhs = cat([c["vt_e"] - keep(lo, ut), c["vt_o"] - keep(hi, ut)], axis=1)
        kb = cat([keep(lo, c["be"]), keep(lo, c["ka"]), keep(hi, c["ka"]), keep(hi, c["be"])], axis=0)
        c["ds"] = jnp.dot(lhs, kb, preferred_element_type=F32)
    for c in chains:
        c["d"]["s_ref"][c["d"]["bi"], c["pi"]] = (c["S"] + c["ds"]) * c["d"]["ops"]["p_end"][:, c["sl"]]
    for d in dirs:
        d["y_ref"][d["bi"], d["rows"], :] =cat([c["y"] for c in chains if c["d"] is d], axis=1).astype(d["y_ref"].dtype)


def _wkv_kernel(rf_ref, kf_ref, vf_ref, kkf_ref, af_ref, lwf_ref,
                rb_ref, kb_ref, vb_ref, kkb_ref, ab_ref, lwb_ref, ka_ref,
                yf_ref, yb_ref, sf_ref, sb_ref, *, n_batch, n_heads, n_chunks):
    @pl.when(pl.program_id(2) == 0)
    def _():
        sf_ref[...] = jnp.zeros_like(sf_ref)
        sb_ref[...] = jnp.zeros_like(sb_ref)

    C = WKV_CHUNK

    def body(j, carry):
        jb = n_chunks - 1 - j
        rows_f = pl.ds(pl.multiple_of(j * C, C), C)
        rows_b = pl.ds(pl.multiple_of(jb * C, C), C)
        dirs = []
        for bi in range(n_batch):
            dirs.append(dict(bi=bi, rows=rows_f, y_ref=yf_ref, s_ref=sf_ref,
                             ops=_wkv_operands(bi, rows_f, False, rf_ref, kf_ref, vf_ref, kkf_ref,
                                               af_ref, lwf_ref, ka_ref)))
            dirs.append(dict(bi=bi, rows=rows_b, y_ref=yb_ref, s_ref=sb_ref,
                             ops=_wkv_operands(bi, rows_b, True, rb_ref, kb_ref, vb_ref, kkb_ref,
                                               ab_ref, lwb_ref, ka_ref)))
        _wkv_step(dirs, n_heads // 2)
        return carry

    lax.fori_loop(0, n_chunks, body, 0, unroll=min(WKV_UNROLL, n_chunks))


def _wkv(r, k, v, kk, a_f, a_b, lw_f, lw_b, k_a):
    B, T, _ = r.shape
    tt = min(TILE_WKV, T)
    hg = WKV_HEADS_PER_STEP
    nb = WKV_BATCH_PER_STEP if B % WKV_BATCH_PER_STEP == 0 else 1
    nt = T // tt
    ng = RWKV_HEADS // hg
    w = hg * HEAD_DIM
    cpt = tt // WKV_CHUNK
    fwd = pl.BlockSpec((nb, tt, w), lambda b, g, i: (b, i, g))
    bwd = pl.BlockSpec((nb, tt, w), lambda b, g, i: (b, nt - 1 - i, g))
    ka_spec = pl.BlockSpec((1, w), lambda b, g, i: (0, g))
    kern = functools.partial(_wkv_kernel, n_batch=nb, n_heads=hg, n_chunks=cpt)
    return pl.pallas_call(
        kern,
        grid=(B // nb, ng, nt),
        in_specs=[fwd] * 6 + [bwd] * 6 + [ka_spec],
        out_specs=[fwd, bwd],
        out_shape=[jax.ShapeDtypeStruct((B, T, D_RWKV), BF16)] * 2,
        scratch_shapes=[pltpu.VMEM((nb, hg // 2, HEAD_DIM, 2 * HEAD_DIM), F32)] * 2,
        compiler_params=pltpu.CompilerParams(
            dimension_semantics=("parallel", "parallel", "arbitrary"),
            vmem_limit_bytes=VMEM_LIMIT),
        name="wkv",
    )(r, k, v, kk, a_f, lw_f, r, k, v, kk, a_b, lw_b, k_a)


def _mla_prep_kernel(z_ref, qg_ref, kvg_ref, wq_ref, wk_ref, wvt_ref, e2_ref,
                     cosq_ref, sinq_ref, cs_ref, q_out, k_out, vt_out):
    z = z_ref[0].astype(F32)
    c_q = z[:, :Q_LORA_RANK]
    c_kv = z[:, Q_LORA_RANK:Q_LORA_RANK + KV_LORA_RANK]
    kr = z[:, Q_LORA_RANK + KV_LORA_RANK:]
    nq = _rms(c_q, qg_ref[...]).astype(BF16)
    nkv = _rms(c_kv, kvg_ref[...]).astype(BF16)
    q = jnp.dot(nq, wq_ref[...], preferred_element_type=F32)
    kf = (jnp.dot(nkv, wk_ref[...], preferred_element_type=F32)
          + _dot(kr * cs_ref[...], e2_ref[...]))
    vt = lax.dot_general(wvt_ref[...], nkv, (((1,), (1,)), ((), ())), preferred_element_type=F32)
    cosq = cosq_ref[...]
    sinq = sinq_ref[...]
    vt_out[0] = vt.astype(vt_out.dtype)
    for h in range(MLA_HEADS):
        qh = q[:, h * QK_PAD:(h + 1) * QK_PAD]
        q_rot = pltpu.roll(qh, QK_PAD - MLA_ROPE_DIM, 1)
        q_out[0, h] = (qh * cosq + q_rot * sinq).astype(q_out.dtype)
        k_out[0, h] = kf[:, h * QK_PAD:(h + 1) * QK_PAD].astype(k_out.dtype)


def _mla_prep(z_m, params, tabs):
    B, T, _ = z_m.shape
    tt = min(TILE_MLA_PREP, T)
    full = lambda a: pl.BlockSpec(a.shape, lambda b, i: (0,) * a.ndim)
    tab = lambda a: pl.BlockSpec((tt, a.shape[1]), lambda b, i: (i, 0))
    hspec = pl.BlockSpec((1, MLA_HEADS, tt, QK_PAD), lambda b, i: (b, 0, i, 0))
    vt_rows = D_MLA
    return pl.pallas_call(
        _mla_prep_kernel,
        grid=(B, T // tt),
        in_specs=[pl.BlockSpec((1, tt, MLA_COLS_PAD), lambda b, i: (b, i, 0))]
        + [full(p) for p in params] + [tab(t) for t in tabs],
        out_specs=[hspec, hspec, pl.BlockSpec((1, vt_rows, tt), lambda b, i: (b, 0, i))],
        out_shape=[jax.ShapeDtypeStruct((B, MLA_HEADS, T, QK_PAD), BF16),
                   jax.ShapeDtypeStruct((B, MLA_HEADS, T, QK_PAD), BF16),
                   jax.ShapeDtypeStruct((B, vt_rows, T), BF16)],
        compiler_params=pltpu.CompilerParams(dimension_semantics=("parallel", "parallel"),
                                             vmem_limit_bytes=VMEM_LIMIT),
        name="mla_prep",
    )(z_m, *params, *tabs)


def _mix_math(y_f, y_b, bonus, g, y_mla16, x, lng, lnb, ones_bd, wo_r, wo_m, fg):
    y = y_f.astype(F32) + y_b.astype(F32)
    inv_n = 1.0 / HEAD_DIM
    mu = _dot(y, ones_bd) * inv_n
    yc = y - mu
    var = _dot(yc * yc, ones_bd) * inv_n
    gn = yc * lax.rsqrt(var + GN_EPS) * lng + lnb
    y_rwkv = (gn + bonus.astype(F32)) * g.astype(F32)
    h = x + _dot(y_rwkv, wo_r) + jnp.dot(y_mla16, wo_m, preferred_element_type=F32)
    return h, _rms(h, fg)


def _attn_mix_kernel(q_ref, k_ref, vt_ref, og_ref, yf_ref, yb_ref, bonus_ref, g_ref, x_ref,
                     lng_ref, lnb_ref, ones_ref, wo_r_ref, wo_m_ref, fg_ref, h_out, n_out):
    def scores(h):
        return _dot_nt(k_ref[0, h], q_ref[0, h])

    outs = []
    s_next = scores(0)
    for h in range(MLA_HEADS):
        s = s_next
        if h + 1 < MLA_HEADS:
            s_next = scores(h + 1)
        m = jnp.max(s, axis=0, keepdims=True)
        p = jnp.exp2(s - m)
        l = jnp.sum(p, axis=0, keepdims=True)
        o = jnp.dot(vt_ref[0, h * MLA_V_DIM:(h + 1) * MLA_V_DIM, :], p.astype(BF16),
                    preferred_element_type=F32)
        outs.append(o / l)
    o = jnp.concatenate(outs, axis=0)
    y = o * lax.rsqrt(jnp.mean(o * o, axis=0, keepdims=True) + NORM_EPS) * og_ref[...]
    y_mla = y.T.astype(BF16)
    h, n = _mix_math(yf_ref[0], yb_ref[0], bonus_ref[0], g_ref[0], y_mla, x_ref[0],
                     lng_ref[...], lnb_ref[...], ones_ref[...], wo_r_ref[...], wo_m_ref[...], fg_ref[...])
    h_out[0] = h
    n_out[0] = n.astype(n_out.dtype)


def _attn_mix(q, k, vt, out_g_col, y_f, y_b, bonus, g, x, params):
    B, H, T, _ = q.shape
    tq = min(TILE_ATTN_Q, T)
    full = lambda a: pl.BlockSpec(a.shape, lambda b, i: (0,) * a.ndim)
    tok = lambda w: pl.BlockSpec((1, tq, w), lambda b, i: (b, i, 0))
    return pl.pallas_call(
        _attn_mix_kernel,
        grid=(B, T // tq),
        in_specs=[pl.BlockSpec((1, H, tq, QK_PAD), lambda b, i: (b, 0, i, 0)),
                  pl.BlockSpec((1, H, T, QK_PAD), lambda b, i: (b, 0, 0, 0)),
                  pl.BlockSpec((1, vt.shape[1], T), lambda b, i: (b, 0, 0)),
                  full(out_g_col)] + [tok(D_RWKV)] * 4 + [tok(D_MODEL)] + [full(p) for p in params],
        out_specs=[tok(D_MODEL), tok(D_MODEL)],
        out_shape=[jax.ShapeDtypeStruct((B, T, D_MODEL), F32),
                   jax.ShapeDtypeStruct((B, T, D_MODEL), BF16)],
        compiler_params=pltpu.CompilerParams(dimension_semantics=("parallel", "arbitrary"),
                                             vmem_limit_bytes=VMEM_LIMIT),
        name="attn_mix",
    )(q, k, vt, out_g_col, y_f, y_b, bonus, g, x, *params)


def _ffn_kernel(n_ref, h_ref, wg_ref, wv_ref, cg_ref, cv_ref, bg_ref, bv_ref, wd_ref, fg_ref,
                o_ref, npad_ref, act_ref):
    i = pl.program_id(1)
    T = n_ref.shape[1]
    R = o_ref.shape[1]
    H = FFN_HALO

    @pl.when(i == 0)
    def _():
        zeros = jnp.zeros((H, D_MODEL), npad_ref.dtype)
        npad_ref[0:H, :] = zeros
        npad_ref[H + T:H + T + H, :] = zeros
        npad_ref[H:H + T, :] = n_ref[0]

    n_win = npad_ref[pl.ds(pl.multiple_of(i * R, R), R + 2 * H), :]

    def conv(u, c, b):
        return c[0:1] * u[H - 1:H - 1 + R] + c[1:2] * u[H:H + R] + c[2:3] * u[H + 1:H + 1 + R] + b

    fc = FFN_COL_CHUNK
    n_chunks = D_FF // fc
    f = None
    g0 = 0
    for j in range(n_chunks):
        cs = slice(j * fc, (j + 1) * fc)
        gate = conv(jnp.dot(n_win, wg_ref[:, cs], preferred_element_type=F32), cg_ref[:, cs], bg_ref[:, cs])
        val = conv(jnp.dot(n_win, wv_ref[:, cs], preferred_element_type=F32), cv_ref[:, cs], bv_ref[:, cs])
        act_ref[:, cs] = (gate * _sigmoid(gate) * val).astype(act_ref.dtype)
        if (j + 1) % FFN_DOWN_GROUP == 0 or j + 1 == n_chunks:
            gs = slice(g0 * fc, (j + 1) * fc)
            part = jnp.dot(act_ref[:, gs], wd_ref[gs, :], preferred_element_type=F32)
            f = part if f is None else f + part
            g0 = j + 1
    o_ref[0] = _rms(h_ref[0] + f, fg_ref[...])


def _ffn(n2, h1, w_up, conv_w, conv_b, w_down, final_g):
    B, T, _ = n2.shape
    R = min(TILE_FFN, T)
    once = lambda a: pl.BlockSpec(a.shape, lambda b, i: (0,) * a.ndim, pipeline_mode=pl.Buffered(1))
    half = lambda a, j: pl.BlockSpec((a.shape[0], D_FF), lambda b, i: (0, j), pipeline_mode=pl.Buffered(1))
    rows = pl.BlockSpec((1, R, D_MODEL), lambda b, i: (b, i, 0))
    return pl.pallas_call(
        _ffn_kernel,
        grid=(B, T // R),
        in_specs=[pl.BlockSpec((1, T, D_MODEL), lambda b, i: (b, 0, 0)), rows,
                  half(w_up, 0), half(w_up, 1), half(conv_w, 0), half(conv_w, 1),
                  half(conv_b, 0), half(conv_b, 1), once(w_down), once(final_g)],
        out_specs=rows,
        out_shape=jax.ShapeDtypeStruct((B, T, D_MODEL), F32),
        scratch_shapes=[pltpu.VMEM((T + 2 * FFN_HALO, D_MODEL), BF16),
                        pltpu.VMEM((R, D_FF), BF16)],
        compiler_params=pltpu.CompilerParams(dimension_semantics=("parallel", "arbitrary"),
                                             vmem_limit_bytes=VMEM_LIMIT),
        name="ffn",
    )(n2, h1, w_up, w_up, conv_w, conv_w, conv_b, conv_b, w_down, final_g)


def _rot_cols(w):
    half = w.shape[-1] // 2
    return jnp.concatenate([-w[..., half:], w[..., :half]], axis=-1)


def _block_diag2(a, b):
    za = jnp.zeros((a.shape[0], b.shape[1]), a.dtype)
    zb = jnp.zeros((b.shape[0], a.shape[1]), b.dtype)
    return jnp.concatenate([jnp.concatenate([a, za], axis=1), jnp.concatenate([zb, b], axis=1)], axis=0)


def _shift_diff_matrices(n):
    eye = np.eye(n, dtype=np.float32)
    d_prev = np.eye(n, k=-1, dtype=np.float32) - eye
    d_next = np.eye(n, k=1, dtype=np.float32) - eye
    return jnp.asarray(d_prev, dtype=BF16), jnp.asarray(d_next, dtype=BF16)


def _head_ones():
    idx = np.arange(D_RWKV) // HEAD_DIM
    return jnp.asarray((idx[:, None] == idx[None, :]).astype(np.float32), dtype=BF16)


def _rope_tables(T):
    inv_freq = jnp.power(ROPE_THETA, -jnp.arange(0, MLA_ROPE_DIM, 2, dtype=F32) / MLA_ROPE_DIM)
    ang = jnp.arange(T, dtype=F32)[:, None] * inv_freq[None, :]
    ang = jnp.concatenate([ang, ang], axis=-1)
    cos, sin = jnp.cos(ang), jnp.sin(ang)
    one = jnp.ones((T, MLA_NOPE_DIM), F32)
    zn = jnp.zeros((T, MLA_NOPE_DIM), F32)
    zp = jnp.zeros((T, QK_PAD - MLA_NOPE_DIM - MLA_ROPE_DIM), F32)
    q_scale = MLA_SCALE * math.log2(math.e)
    cosq = jnp.concatenate([one, cos, zp], axis=1) * q_scale
    sinq = jnp.concatenate([zn, sin, zp], axis=1) * q_scale
    cs = jnp.concatenate([cos, sin, jnp.zeros((T, 128 - 2 * MLA_ROPE_DIM), F32)], axis=1)
    return cosq, sinq, cs


def _pad_heads(w, src_width, src_off, n, dst_off):
    K = w.shape[0]
    w = w.reshape(K, MLA_HEADS, src_width)[:, :, src_off:src_off + n]
    out = jnp.zeros((K, MLA_HEADS, QK_PAD), w.dtype)
    out = out.at[:, :, dst_off:dst_off + n].set(w)
    return out.reshape(K, MLA_HEADS * QK_PAD)


def kernel(x, ln_mix_g, w_in, shift_mu_prev, shift_mu_next, decay_w0_fwd, decay_w2_fwd, decay_w0_bwd, decay_w2_bwd, iclr_a0_fwd, iclr_a2_fwd, iclr_a0_bwd, iclr_a2_bwd, gate_g2, k_k, k_a, r_k, ln_x_g, ln_x_b, q_norm_g, w_uq, kv_norm_g, w_ukv, mla_out_g, w_out, ln_ffn_g, w_ffn_up, ffn_conv_w, ffn_conv_b, w_ffn_down, ln_final_g):
    B, T, D = x.shape
    assert w_in.shape[0] == 1, "single-layer block"
    n_tok = B * T
    row = lambda a: a.reshape(1, -1).astype(F32)
    l = 0

    w_in_l = w_in[l]
    w_r = w_in_l[:, :RWKV_COLS].astype(BF16)
    w_kr = w_in_l[:, RWKV_COLS + Q_LORA_RANK + KV_LORA_RANK:]
    w_m = jnp.concatenate([w_in_l[:, RWKV_COLS:], _rot_cols(w_kr),
                           jnp.zeros((D, MLA_COLS_PAD - (Q_LORA_RANK + KV_LORA_RANK + 2 * MLA_ROPE_DIM)), F32)],
                          axis=1).astype(BF16)
    ones_bd = _head_ones()
    prep_params = [
        row(shift_mu_prev[l]), row(shift_mu_next[l]),
        row(jnp.concatenate([decay_w0_fwd[l], decay_w0_bwd[l]])),
        _block_diag2(decay_w2_fwd[l], decay_w2_bwd[l]).astype(BF16),
        row(jnp.concatenate([iclr_a0_fwd[l], iclr_a0_bwd[l]])),
        _block_diag2(iclr_a2_fwd[l], iclr_a2_bwd[l]).astype(BF16),
        gate_g2[l].astype(BF16),
        row(k_k[l]), row(k_a[l]), row(r_k[l]), ones_bd, *_shift_diff_matrices(min(RWKV_SHIFT_BLOCK, T)),
    ]
    qw = MLA_NOPE_DIM + MLA_ROPE_DIM
    w_uq_l = w_uq[l]
    w_uq_h = w_uq_l.reshape(Q_LORA_RANK, MLA_HEADS, qw)
    wq_all = jnp.concatenate([w_uq_h, _rot_cols(w_uq_h[:, :, MLA_NOPE_DIM:])], axis=-1)
    wq_all = wq_all.reshape(Q_LORA_RANK, MLA_HEADS * QK_PAD).astype(BF16)
    kvw = MLA_NOPE_DIM + MLA_V_DIM
    w_ukv_l = w_ukv[l]
    wk_pad = _pad_heads(w_ukv_l, kvw, 0, MLA_NOPE_DIM, 0).astype(BF16)
    wv_t = w_ukv_l.reshape(KV_LORA_RANK, MLA_HEADS, kvw)[:, :, MLA_NOPE_DIM:].reshape(KV_LORA_RANK, D_MLA).T.astype(BF16)
    e2 = np.zeros((128, MLA_HEADS * QK_PAD), np.float32)
    for h in range(MLA_HEADS):
        for j in range(MLA_ROPE_DIM):
            e2[j, h * QK_PAD + MLA_NOPE_DIM + j] = 1.0
            e2[MLA_ROPE_DIM + j, h * QK_PAD + MLA_NOPE_DIM + j] = 1.0
    e2 = jnp.asarray(e2, dtype=BF16)
    cosq, sinq, cs = _rope_tables(T)
    mla_params = [row(q_norm_g[l]), row(kv_norm_g[l]), wq_all, wk_pad, wv_t, e2]
    w_out_l = w_out[l]
    mix_params = [row(ln_x_g[l]), row(ln_x_b[l]), ones_bd,
                  w_out_l[:D_RWKV].astype(BF16), w_out_l[D_RWKV:].astype(BF16), row(ln_ffn_g[l])]
    w_up = w_ffn_up[l]
    conv_w = ffn_conv_w[l]
    conv_b = ffn_conv_b[l]

    x2 = x.reshape(n_tok, D)
    z_r, z_m = _inproj(x2, row(ln_mix_g[l]), w_r, w_m)
    z_r = z_r.reshape(B, T, RWKV_COLS)
    z_m = z_m.reshape(B, T, MLA_COLS_PAD)

    r, k, v, kk, a_f, a_b, lw_f, lw_b, bonus, g = _rwkv_prep(z_r, prep_params)
    y_f, y_b = _wkv(r, k, v, kk, a_f, a_b, lw_f, lw_b, row(k_a[l]))

    q, kf, vt_mla = _mla_prep(z_m, mla_params, [cosq, sinq, cs])
    h1, n2 = _attn_mix(q, kf, vt_mla, mla_out_g[l].reshape(D_MLA, 1).astype(F32),
                       y_f, y_b, bonus, g, x, mix_params)

    return _ffn(n2, h1,
                w_up.astype(BF16), conv_w, row(conv_b), w_ffn_down[l].astype(BF16), row(ln_final_g))
```
